```python
import math
import jax, jax.numpy as jnp
from jax import lax
import numpy as np

D_MODEL = 2048
BATCH = 16
SEQ = 256
DEPTH = 4
DEC_BATCH = 4
DEC_SEQ = 2048
PAST_LEN = 256

GRID_W = 64
CONV_WIDTH = 512
CONV_K = 31
DA_HEADS = 8
DA_QK_DIM = 64
DA_V_DIM = 128
DA_WIDTH = DA_HEADS * DA_V_DIM
DA_QK_COLS = DA_HEADS * 2 * DA_QK_DIM
RG_WIDTH = 512
RG_HEADS = 8
RG_HEAD_DIM = RG_WIDTH // RG_HEADS
RG_CONV_K = 4
RG_C = 8.0
MIX_WIDTH = CONV_WIDTH + DA_WIDTH + RG_WIDTH
IN_SPLITS = (CONV_WIDTH, CONV_WIDTH, DA_QK_COLS, DA_QK_COLS, DA_WIDTH, RG_WIDTH, RG_WIDTH)
IN_WIDTH = 2 * CONV_WIDTH + 2 * DA_QK_COLS + DA_WIDTH + 2 * RG_WIDTH
N_GROUPS = 4
EXPERTS_PER_GROUP = 8
N_EXPERTS = N_GROUPS * EXPERTS_PER_GROUP
TOP_K = 2
D_EXPERT = 512
ROPE_BASE = 10000.0
ROPE_PAIRS_PER_AXIS = DA_QK_DIM // 4
Q_BLOCK = 128
EXPERT_BLOCK = 128
EPS = 1e-6

kernel_name = "hybrid_dit_conv_diffattn_rglru_hmoe_step"


def rmsnorm(x, g):
    xf = x.astype(jnp.float32)
    y = xf * lax.rsqrt(jnp.mean(xf * xf, axis=-1, keepdims=True) + EPS)
    return (y * g.astype(jnp.float32)).astype(x.dtype)


def diff_lambda_init(layer):
    return 0.8 - 0.6 * math.exp(-0.3 * layer)


def depthwise_conv(x, w, b, pad):
    C = x.shape[-1]
    y = lax.conv_general_dilated(x, w.astype(x.dtype)[:, None, :], window_strides=(1,), padding=[pad],
                                 dimension_numbers=("NWC", "WIO", "NWC"), feature_group_count=C)
    return y + b.astype(x.dtype)


def axial_rope(n_tokens):
    rows = n_tokens // GRID_W
    row_ids = jnp.repeat(jnp.arange(rows, dtype=jnp.float32), GRID_W)
    col_ids = jnp.tile(jnp.arange(GRID_W, dtype=jnp.float32), rows)
    inv = 1.0 / (ROPE_BASE ** (jnp.arange(ROPE_PAIRS_PER_AXIS, dtype=jnp.float32) / ROPE_PAIRS_PER_AXIS))
    ang = jnp.concatenate([row_ids[:, None] * inv, col_ids[:, None] * inv], axis=-1)
    return jnp.cos(ang), jnp.sin(ang)


def apply_axial_rope(x, cos, sin):
    half = DA_QK_DIM // 2
    xf = x.astype(jnp.float32)
    x1, x2 = xf[..., :half], xf[..., half:]
    c = cos[None, :, None, None, :]
    s = sin[None, :, None, None, :]
    return jnp.concatenate([x1 * c - x2 * s, x2 * c + x1 * s], axis=-1).astype(x.dtype)


def conformer_conv(a, g, dw_w, dw_b, ln_g, ln_b):
    u = a * jax.nn.sigmoid(g)
    u = depthwise_conv(u, dw_w, dw_b, (CONV_K // 2, CONV_K // 2))
    uf = u.astype(jnp.float32)
    mu = jnp.mean(uf, axis=-1, keepdims=True)
    var = jnp.mean(jnp.square(uf - mu), axis=-1, keepdims=True)
    un = (uf - mu) * lax.rsqrt(var + EPS) * ln_g.astype(jnp.float32) + ln_b.astype(jnp.float32)
    return jax.nn.silu(un).astype(a.dtype)


def diff_attention(q, k, v, lam):
    B, Sq = q.shape[0], q.shape[1]
    nblk = Sq // Q_BLOCK
    qb = jnp.moveaxis(q.reshape(B, nblk, Q_BLOCK, DA_HEADS, 2, DA_QK_DIM), 1, 0)
    scale = DA_QK_DIM ** -0.5

    def block(qblk):
        s = jnp.einsum("bqhcd,bkhcd->bchqk", qblk, k).astype(jnp.float32) * scale
        p = jax.nn.softmax(s, axis=-1)
        a = p[:, 0] - lam * p[:, 1]
        return jnp.einsum("bhqk,bkhd->bqhd", a.astype(v.dtype), v)

    o = lax.map(block, qb)
    return jnp.moveaxis(o, 0, 1).reshape(B, Sq, DA_HEADS, DA_V_DIM)


def _linear_combine(e1, e2):
    a1, b1 = e1
    a2, b2 = e2
    return a1 * a2, a2 * b1 + b2


def rglru_direction(x, conv_w, conv_b, w_a, b_a, w_x, b_x, lam, h0, reverse):
    B, S, C = x.shape
    pad = (0, RG_CONV_K - 1) if reverse else (RG_CONV_K - 1, 0)
    xc = depthwise_conv(x, conv_w, conv_b, pad)
    xh = xc.reshape(B, S, RG_HEADS, RG_HEAD_DIM)
    r = jax.nn.sigmoid((jnp.einsum("bshi,hij->bshj", xh, w_a).reshape(B, S, C) + b_a).astype(jnp.float32))
    i = jax.nn.sigmoid((jnp.einsum("bshi,hij->bshj", xh, w_x).reshape(B, S, C) + b_x).astype(jnp.float32))
    log_a = -RG_C * r * jax.nn.softplus(-lam.astype(jnp.float32))
    a = jnp.exp(log_a)
    u = jnp.sqrt(-jnp.expm1(2.0 * log_a)) * i * xc.astype(jnp.float32)
    if reverse:
        a, u = jnp.flip(a, 1), jnp.flip(u, 1)
    a_cum, b_cum = lax.associative_scan(_linear_combine, (a, u), axis=1)
    h = a_cum * h0.astype(jnp.float32)[:, None, :] + b_cum
    last = h[:, -1]
    if reverse:
        h = jnp.flip(h, 1)
    return h, last


def token_mixers(h, lp, lam_init, ctx):
    B, S, _ = h.shape
    cuts = np.cumsum(IN_SPLITS)[:-1].tolist()
    ca, cg, q, k, v, rx, rz = jnp.split(h @ lp["w_in"], cuts, axis=-1)

    conv_out = conformer_conv(ca, cg, lp["conv_dw_w"], lp["conv_dw_b"], lp["conv_ln_g"], lp["conv_ln_b"])

    q = q.reshape(B, S, DA_HEADS, 2, DA_QK_DIM)
    k = k.reshape(B, S, DA_HEADS, 2, DA_QK_DIM)
    v = v.reshape(B, S, DA_HEADS, DA_V_DIM)
    if ctx is None:
        q_all, k_all, v_all = q, k, v
        h0 = jnp.zeros((B, 2, RG_WIDTH), jnp.float32)
    else:
        ctx_k, ctx_v, h0, cos, sin = ctx
        q_all = apply_axial_rope(q, cos, sin)
        k_ctx = ctx_k.reshape(B, ctx_k.shape[1], DA_HEADS, 2, DA_QK_DIM).astype(k.dtype)
        k_all = jnp.concatenate([apply_axial_rope(k, cos, sin), k_ctx], axis=1)
        v_all = jnp.concatenate([v, ctx_v.astype(v.dtype)], axis=1)
    lvec = lp["da_lambda"].astype(jnp.float32)
    lam = jnp.exp(jnp.sum(lvec[0] * lvec[1])) - jnp.exp(jnp.sum(lvec[2] * lvec[3])) + lam_init
    o = diff_attention(q_all, k_all, v_all, lam)
    da_out = (rmsnorm(o, lp["da_norm"]) * (1.0 - lam_init)).reshape(B, S, DA_WIDTH).astype(h.dtype)

    hs, lasts = [], []
    for d in range(2):
        hd, last = rglru_direction(rx, lp["rg_conv_w"][d], lp["rg_conv_b"][d], lp["rg_w_a"][d], lp["rg_b_a"][d],
                                   lp["rg_w_x"][d], lp["rg_b_x"][d], lp["rg_lambda"][d], h0[:, d], d == 1)
        hs.append(hd)
        lasts.append(last)
    rg_out = ((hs[0] + hs[1]) * jax.nn.gelu(rz.astype(jnp.float32))).astype(h.dtype)

    mixed = jnp.concatenate([conv_out, da_out, rg_out], axis=-1) @ lp["w_out"]
    return mixed, k.reshape(B, S, DA_HEADS, 2 * DA_QK_DIM), v, jnp.stack(lasts, axis=1)


def routed_experts(x, experts, gates, w_gate, w_up, w_down):
    T = x.shape[0]
    A = T * TOP_K
    e_flat = experts.reshape(A)
    tok_flat = jnp.repeat(jnp.arange(T, dtype=jnp.int32), TOP_K)
    order = jnp.argsort(e_flat)
    e_sorted = e_flat[order]
    tok_sorted = tok_flat[order]
    gate_sorted = gates.reshape(A)[order]
    counts = jnp.bincount(e_flat, length=N_EXPERTS)
    padded = (counts + EXPERT_BLOCK - 1) // EXPERT_BLOCK * EXPERT_BLOCK
    start = jnp.cumsum(counts) - counts
    ends_p = jnp.cumsum(padded)
    pstart = ends_p - padded
    dest = pstart[e_sorted] + jnp.arange(A, dtype=jnp.int32) - start[e_sorted]
    n_blocks = -(-A // EXPERT_BLOCK) + N_EXPERTS
    buf_tok = jnp.zeros((n_blocks * EXPERT_BLOCK,), jnp.int32).at[dest].set(tok_sorted)
    blk_expert = jnp.minimum(jnp.searchsorted(ends_p, jnp.arange(n_blocks) * EXPERT_BLOCK, side="right"),
                             N_EXPERTS - 1)
    xb = x[buf_tok].reshape(n_blocks, EXPERT_BLOCK, x.shape[1])

    def expert_block(args):
        xblk, e = args
        hid = jax.nn.silu(xblk @ w_gate[e]) * (xblk @ w_up[e])
        return hid @ w_down[e]

    yb = lax.map(expert_block, (xb, blk_expert)).reshape(n_blocks * EXPERT_BLOCK, x.shape[1])
    y_assign = yb[dest] * gate_sorted[:, None].astype(yb.dtype)
    return jax.ops.segment_sum(y_assign, tok_sorted, num_segments=T)


def hier_moe(h, lp):
    B, S, D = h.shape
    x = h.reshape(B * S, D)
    grp_logits = (x @ lp["moe_w_grp"]).astype(jnp.float32) + lp["moe_b_grp"].astype(jnp.float32)
    grp = jnp.argmax(grp_logits, axis=-1)
    p_grp = jnp.max(jax.nn.softmax(grp_logits, axis=-1), axis=-1, keepdims=True)
    exp_logits = ((x @ lp["moe_w_exp"]).astype(jnp.float32) + lp["moe_b_exp"].astype(jnp.float32))
    exp_logits = exp_logits.reshape(B * S, N_GROUPS, EXPERTS_PER_GROUP)
    in_grp = jnp.einsum("tg,tge->te", jax.nn.one_hot(grp, N_GROUPS, dtype=jnp.float32), exp_logits)
    w_top, i_top = lax.top_k(jax.nn.softmax(in_grp, axis=-1), TOP_K)
    gates = w_top / jnp.sum(w_top, axis=-1, keepdims=True) * p_grp
    experts = grp[:, None].astype(jnp.int32) * EXPERTS_PER_GROUP + i_top.astype(jnp.int32)
    y = routed_experts(x, experts, gates, lp["moe_w_gate"], lp["moe_w_up"], lp["moe_w_down"])
    return y.reshape(B, S, D)


def trunk_layer(x, mod, lp, lam_init, ctx):
    shift1, scale1, gate1, shift2, scale2, gate2 = jnp.split(mod, 6, axis=-1)
    h = (rmsnorm(x, lp["norm1"]) * (1.0 + scale1) + shift1).astype(x.dtype)
    mixed, k, v, st = token_mixers(h, lp, lam_init, ctx)
    x = x + gate1 * mixed
    h2 = (rmsnorm(x, lp["norm2"]) * (1.0 + scale2) + shift2).astype(x.dtype)
    x = x + gate2 * hier_moe(h2, lp)
    return x, k, v, st


def setup_inputs(seed: int = 0) -> dict:
    key = jax.random.key(seed)
    k = jax.random.split(key, 36)
    f32 = jnp.float32
    D = D_MODEL

    def nrm(i, shape, scale):
        return jax.random.normal(k[i], shape, f32) * scale

    rg_u = jax.random.uniform(k[26], (DEPTH, 2, RG_WIDTH), f32, 0.9, 0.999)
    rg_a = rg_u ** (1.0 / RG_C)
    return {
        "x_prompt": nrm(0, (BATCH, SEQ, D), 1.0),
        "x_sample": nrm(1, (DEC_BATCH, DEC_SEQ, D), 1.0),
        "c": nrm(2, (DEC_BATCH, D), 1.0),
        "c_ctx": nrm(3, (D,), 1.0),
        "cache_k": nrm(4, (DEC_BATCH, DEPTH, PAST_LEN, DA_HEADS, 2 * DA_QK_DIM), 1.0),
        "cache_v": nrm(5, (DEC_BATCH, DEPTH, PAST_LEN, DA_HEADS, DA_V_DIM), 1.0),
        "state_rglru": nrm(6, (DEC_BATCH, DEPTH, 2, RG_WIDTH), 0.5),
        "w_mod": nrm(7, (DEPTH, D, 6 * D), D ** -0.5),
        "b_mod": nrm(8, (DEPTH, 6 * D), 0.02),
        "norm1": 1.0 + nrm(9, (DEPTH, D), 0.02),
        "norm2": 1.0 + nrm(10, (DEPTH, D), 0.02),
        "w_in": nrm(11, (DEPTH, D, IN_WIDTH), D ** -0.5),
        "w_out": nrm(12, (DEPTH, MIX_WIDTH, D), MIX_WIDTH ** -0.5),
        "conv_dw_w": nrm(13, (DEPTH, CONV_K, CONV_WIDTH), CONV_K ** -0.5),
        "conv_dw_b": nrm(14, (DEPTH, CONV_WIDTH), 0.02),
        "conv_ln_g": 1.0 + nrm(15, (DEPTH, CONV_WIDTH), 0.02),
        "conv_ln_b": nrm(16, (DEPTH, CONV_WIDTH), 0.02),
        "da_lambda": nrm(17, (DEPTH, 4, DA_QK_DIM), 0.1),
        "da_norm": 1.0 + nrm(18, (DEPTH, DA_V_DIM), 0.02),
        "rg_conv_w": nrm(19, (DEPTH, 2, RG_CONV_K, RG_WIDTH), RG_CONV_K ** -0.5),
        "rg_conv_b": nrm(20, (DEPTH, 2, RG_WIDTH), 0.02),
        "rg_w_a": nrm(21, (DEPTH, 2, RG_HEADS, RG_HEAD_DIM, RG_HEAD_DIM), RG_HEAD_DIM ** -0.5),
        "rg_b_a": nrm(22, (DEPTH, 2, RG_WIDTH), 0.02),
        "rg_w_x": nrm(23, (DEPTH, 2, RG_HEADS, RG_HEAD_DIM, RG_HEAD_DIM), RG_HEAD_DIM ** -0.5),
        "rg_b_x": nrm(24, (DEPTH, 2, RG_WIDTH), 0.02),
        "rg_lambda": jnp.log(rg_a) - jnp.log1p(-rg_a),
        "moe_w_grp": nrm(25, (DEPTH, D, N_GROUPS), D ** -0.5),
        "moe_b_grp": nrm(27, (DEPTH, N_GROUPS), 0.01),
        "moe_w_exp": nrm(28, (DEPTH, D, N_EXPERTS), D ** -0.5),
        "moe_b_exp": nrm(29, (DEPTH, N_EXPERTS), 0.01),
        "moe_w_gate": nrm(30, (DEPTH, N_EXPERTS, D, D_EXPERT), D ** -0.5),
        "moe_w_up": nrm(31, (DEPTH, N_EXPERTS, D, D_EXPERT), D ** -0.5),
        "moe_w_down": nrm(32, (DEPTH, N_EXPERTS, D_EXPERT, D), D_EXPERT ** -0.5),
        "final_norm": 1.0 + nrm(33, (D,), 0.02),
    }


def reference(x_prompt, x_sample, c, c_ctx, cache_k, cache_v, state_rglru, w_mod, b_mod, norm1, norm2,
              w_in, w_out, conv_dw_w, conv_dw_b, conv_ln_g, conv_ln_b, da_lambda, da_norm,
              rg_conv_w, rg_conv_b, rg_w_a, rg_b_a, rg_w_x, rg_b_x, rg_lambda,
              moe_w_grp, moe_b_grp, moe_w_exp, moe_b_exp, moe_w_gate, moe_w_up, moe_w_down, final_norm):
    cos, sin = axial_rope(x_sample.shape[1])
    xc, xs = x_prompt, x_sample
    new_k, new_v, new_h = [], [], []
    for l in range(DEPTH):
        lp = {
            "w_in": w_in[l], "w_out": w_out[l], "norm1": norm1[l], "norm2": norm2[l],
            "conv_dw_w": conv_dw_w[l], "conv_dw_b": conv_dw_b[l],
            "conv_ln_g": conv_ln_g[l], "conv_ln_b": conv_ln_b[l],
            "da_lambda": da_lambda[l], "da_norm": da_norm[l],
            "rg_conv_w": rg_conv_w[l], "rg_conv_b": rg_conv_b[l], "rg_w_a": rg_w_a[l], "rg_b_a": rg_b_a[l],
            "rg_w_x": rg_w_x[l], "rg_b_x": rg_b_x[l], "rg_lambda": rg_lambda[l],
            "moe_w_grp": moe_w_grp[l], "moe_b_grp": moe_b_grp[l],
            "moe_w_exp": moe_w_exp[l], "moe_b_exp": moe_b_exp[l],
            "moe_w_gate": moe_w_gate[l], "moe_w_up": moe_w_up[l], "moe_w_down": moe_w_down[l],
        }
        lam_init = diff_lambda_init(l)
        mod_ctx = (jax.nn.silu(c_ctx) @ w_mod[l] + b_mod[l])[None, None, :]
        xc, k_l, v_l, h_l = trunk_layer(xc, mod_ctx, lp, lam_init, None)
        new_k.append(k_l)
        new_v.append(v_l)
        new_h.append(h_l)
        mod_lat = (jax.nn.silu(c) @ w_mod[l] + b_mod[l])[:, None, :]
        ctx = (cache_k[:, l], cache_v[:, l], state_rglru[:, l], cos, sin)
        xs = trunk_layer(xs, mod_lat, lp, lam_init, ctx)[0]
    y_prompt = rmsnorm(xc, final_norm)
    y_sample = rmsnorm(xs, final_norm)
    new_cache_k = jnp.stack(new_k, axis=1)
    new_cache_v = jnp.stack(new_v, axis=1)
    new_state_rglru = jnp.stack(new_h, axis=1)
    return (y_prompt, y_sample, new_cache_k, new_cache_v, new_state_rglru)
```

```python
import functools
import math

import jax
import jax.numpy as jnp
from jax import lax
from jax.experimental import pallas as pl
from jax.experimental.pallas import tpu as pltpu

F32 = jnp.float32
BF16 = jnp.bfloat16

D_MODEL = 2048
DEPTH = 4
GRID_W = 64
CONV_WIDTH = 512
CONV_K = 31
DA_HEADS = 8
DA_QK_DIM = 64
DA_V_DIM = 128
DA_WIDTH = DA_HEADS * DA_V_DIM
RG_WIDTH = 512
RG_HEADS = 8
RG_HEAD_DIM = RG_WIDTH // RG_HEADS
RG_CONV_K = 4
RG_C = 8.0
IN_WIDTH = 2 * CONV_WIDTH + 3 * DA_WIDTH + 2 * RG_WIDTH
MIX_WIDTH = CONV_WIDTH + DA_WIDTH + RG_WIDTH
N_GROUPS = 4
EXPERTS_PER_GROUP = 8
N_EXPERTS = N_GROUPS * EXPERTS_PER_GROUP
TOP_K = 2
D_EXPERT = 512
ROPE_BASE = 10000.0
EPS = 1e-6
N_COND = 8
N_MOD = 6

LANE = 128
SUBLANE = 8
COL_Q = (2 * CONV_WIDTH) // LANE
COL_K = COL_Q + DA_WIDTH // LANE
COL_V = COL_K + DA_WIDTH // LANE
COL_RX = (2 * CONV_WIDTH + 3 * DA_WIDTH) // RG_WIDTH
COL_RZ = COL_RX + 1

EXPERT_ROWS = 256
COMBINE_ROWS = 128
VMEM_LIMIT = 56 * 1024 * 1024


def _params(sem, vmem=VMEM_LIMIT):
    return pltpu.CompilerParams(dimension_semantics=sem, vmem_limit_bytes=vmem)


def _norm_mod(x, g, scale, shift):
    y = x * lax.rsqrt(jnp.mean(x * x, axis=-1, keepdims=True) + EPS)
    return (y * g) * (1.0 + scale) + shift


class _Tokens:
    def __init__(self, n_ctx_seq, s_ctx, n_lat_seq, s_lat):
        self.n_ctx_seq, self.s_ctx, self.n_lat_seq, self.s_lat = n_ctx_seq, s_ctx, n_lat_seq, s_lat
        self.t_ctx = n_ctx_seq * s_ctx
        self.t_lat = n_lat_seq * s_lat
        self.t = self.t_ctx + self.t_lat

    def cond(self, i, tm):
        n_ctx_tiles = self.t_ctx // tm
        per_seq = self.s_lat // tm
        return jnp.where(i < n_ctx_tiles, 0, 1 + (i - n_ctx_tiles) // per_seq)

    def mod_spec(self, layer, which, tm):
        return pl.BlockSpec((None, None, None, 1, D_MODEL),
                            lambda i, *_: (layer, self.cond(i, tm), which, 0, 0))


def _mod_kernel(c_ref, w_ref, b_ref, o_ref):
    c = c_ref[...]
    s = c * jax.nn.sigmoid(c)
    o_ref[...] = jnp.dot(s.astype(BF16), w_ref[...].astype(BF16), preferred_element_type=F32) + b_ref[...]


def _modulation(cond, w_mod, b_mod):
    tn = 1024
    n = N_MOD * D_MODEL
    return pl.pallas_call(
        _mod_kernel,
        grid=(DEPTH, n // tn),
        in_specs=[
            pl.BlockSpec((N_COND, D_MODEL), lambda l, j: (0, 0)),
            pl.BlockSpec((None, D_MODEL, tn), lambda l, j: (l, 0, j)),
            pl.BlockSpec((None, 1, tn), lambda l, j: (l, 0, j)),
        ],
        out_specs=pl.BlockSpec((None, N_COND, tn), lambda l, j: (l, 0, j)),
        out_shape=jax.ShapeDtypeStruct((DEPTH, N_COND, n), F32),
        compiler_params=_params(("parallel", "parallel")),
        name="modulation",
    )(cond, w_mod, b_mod.reshape(DEPTH, 1, n))


def _prenorm_kernel(x_ref, g_ref, scale_ref, shift_ref, h_ref):
    h_ref[...] = _norm_mod(x_ref[...], g_ref[...], scale_ref[...], shift_ref[...]).astype(BF16)


def _prenorm(tok, x, norm1, mod5, layer):
    tm = 256
    return pl.pallas_call(
        _prenorm_kernel,
        grid=(tok.t // tm,),
        in_specs=[
            pl.BlockSpec((tm, D_MODEL), lambda i: (i, 0)),
            pl.BlockSpec((None, 1, D_MODEL), lambda i: (layer, 0, 0)),
            tok.mod_spec(layer, 1, tm),
            tok.mod_spec(layer, 0, tm),
        ],
        out_specs=pl.BlockSpec((tm, D_MODEL), lambda i: (i, 0)),
        out_shape=jax.ShapeDtypeStruct((tok.t, D_MODEL), BF16),
        compiler_params=_params(("parallel",)),
        name="prenorm",
    )(x, norm1, mod5, mod5)


def _inproj_kernel(h_ref, w_ref, p_ref):
    p_ref[...] = jnp.dot(h_ref[...], w_ref[...], preferred_element_type=F32)


def _inproj(tok, h, w_in_b, layer):
    tm, tn = 512, 1024
    return pl.pallas_call(
        _inproj_kernel,
        grid=(IN_WIDTH // tn, tok.t // tm),
        in_specs=[
            pl.BlockSpec((tm, D_MODEL), lambda j, i: (i, 0)),
            pl.BlockSpec((None, D_MODEL, tn), lambda j, i: (layer, 0, j)),
        ],
        out_specs=pl.BlockSpec((tm, tn), lambda j, i: (i, j)),
        out_shape=jax.ShapeDtypeStruct((tok.t, IN_WIDTH), F32),
        compiler_params=_params(("parallel", "parallel")),
        name="inproj",
    )(h, w_in_b)


CONV_PAD = 16
CONV_CHUNK = 32
GLU_CHUNK = 128


def _conv_kernel(ca_ref, cg_ref, w_ref, b_ref, g_ref, beta_ref, o_ref, upad_ref, *, seq):
    zeros = jnp.zeros((CONV_PAD, CONV_WIDTH), F32)
    upad_ref[pl.ds(0, CONV_PAD), :] = zeros
    upad_ref[pl.ds(seq + CONV_PAD, CONV_PAD), :] = zeros

    def glu(c, carry):
        r = pl.multiple_of(c * GLU_CHUNK, GLU_CHUNK)
        u = ca_ref[pl.ds(r, GLU_CHUNK), :] * jax.nn.sigmoid(cg_ref[pl.ds(r, GLU_CHUNK), :])
        upad_ref[pl.ds(r + CONV_PAD, GLU_CHUNK), :] = u
        return carry

    lax.fori_loop(0, seq // GLU_CHUNK, glu, 0)

    half = CONV_K // 2

    def conv(c, carry):
        r = pl.multiple_of(c * CONV_CHUNK, CONV_CHUNK)
        acc = jnp.zeros((CONV_CHUNK, CONV_WIDTH), F32) + b_ref[...]
        win = upad_ref[pl.ds(r, CONV_CHUNK + 2 * CONV_PAD), :]
        for j in range(CONV_K):
            o = CONV_PAD - half + j
            acc = acc + w_ref[j:j + 1, :] * win[o:o + CONV_CHUNK, :]
        mu = jnp.mean(acc, axis=-1, keepdims=True)
        d = acc - mu
        var = jnp.mean(d * d, axis=-1, keepdims=True)
        un = d * lax.rsqrt(var + EPS) * g_ref[...] + beta_ref[...]
        o_ref[pl.ds(r, CONV_CHUNK), :] = (un * jax.nn.sigmoid(un)).astype(BF16)
        return carry

    lax.fori_loop(0, seq // CONV_CHUNK, conv, 0)


def _conformer_conv(tok, p, dw_w, dw_b, ln_g, ln_b, layer, latent):
    seq = tok.s_lat if latent else tok.s_ctx
    nseq = tok.n_lat_seq if latent else tok.n_ctx_seq
    row0 = tok.t_ctx // seq if latent else 0
    vec = pl.BlockSpec((None, 1, CONV_WIDTH), lambda i: (layer, 0, 0))
    return pl.pallas_call(
        functools.partial(_conv_kernel, seq=seq),
        grid=(nseq,),
        in_specs=[
            pl.BlockSpec((seq, CONV_WIDTH), lambda i: (row0 + i, 0)),
            pl.BlockSpec((seq, CONV_WIDTH), lambda i: (row0 + i, 1)),
            pl.BlockSpec((None, CONV_K, CONV_WIDTH), lambda i: (layer, 0, 0)),
            vec, vec, vec,
        ],
        out_specs=pl.BlockSpec((seq, CONV_WIDTH), lambda i: (i, 0)),
        out_shape=jax.ShapeDtypeStruct((nseq * seq, CONV_WIDTH), BF16),
        scratch_shapes=[pltpu.VMEM((seq + 2 * CONV_PAD, CONV_WIDTH), F32)],
        compiler_params=_params(("parallel",)),
        name="conformer_conv_lat" if latent else "conformer_conv_ctx",
    )(p, p, dw_w, dw_b, ln_g, ln_b)


ATTN_SCALE = DA_QK_DIM ** -0.5
ROPE_CHUNK = 256


def _rope(x, cos, sin_signed):
    lane = lax.broadcasted_iota(jnp.int32, x.shape, 1)
    first = (lane % DA_QK_DIM) < (DA_QK_DIM // 2)
    partner = jnp.where(first, pltpu.roll(x, LANE - DA_QK_DIM // 2, axis=1), pltpu.roll(x, DA_QK_DIM // 2, axis=1))
    return x * cos + partner * sin_signed


def _lambda(lamv_ref, lam_init):
    lv = lamv_ref[...]
    a = jnp.sum(lv[0:1, :] * lv[1:2, :], axis=-1, keepdims=True)
    b = jnp.sum(lv[2:3, :] * lv[3:4, :], axis=-1, keepdims=True)
    return jnp.exp(a) - jnp.exp(b) + lam_init


def _diff_attn(q, kb, vb, lam, g, lam_init):
    tq = q.shape[0]
    lane = lax.broadcasted_iota(jnp.int32, q.shape, 1)
    qs = q * ATTN_SCALE
    q1 = jnp.where(lane < DA_QK_DIM, qs, 0.0).astype(BF16)
    q2 = jnp.where(lane >= DA_QK_DIM, qs, 0.0).astype(BF16)
    qq = jnp.concatenate([q1, q2], axis=0)
    s = lax.dot_general(qq, kb, (((1,), (1,)), ((), ())), preferred_element_type=F32)
    m = jnp.max(s, axis=-1, keepdims=True)
    e = jnp.exp(s - m)
    p = e * (1.0 / jnp.sum(e, axis=-1, keepdims=True))
    a = p[:tq] - lam * p[tq:]
    o = jnp.dot(a.astype(BF16), vb, preferred_element_type=F32)
    y = o * lax.rsqrt(jnp.mean(o * o, axis=-1, keepdims=True) + EPS) * g
    return (y * (1.0 - lam_init)).astype(BF16)


def _attn_ctx_kernel(q_ref, k_ref, v_ref, lamv_ref, g_ref, o_ref, *, lam_init):
    lam = _lambda(lamv_ref, lam_init)
    o_ref[...] = _diff_attn(q_ref[...], k_ref[...].astype(BF16), v_ref[...].astype(BF16), lam, g_ref[...], lam_init)


def _attn_ctx(tok, p, da_lambda, da_norm, layer, lam_init):
    s = tok.s_ctx
    return pl.pallas_call(
        functools.partial(_attn_ctx_kernel, lam_init=lam_init),
        grid=(tok.n_ctx_seq, DA_HEADS),
        in_specs=[
            pl.BlockSpec((s, LANE), lambda b, h: (b, COL_Q + h)),
            pl.BlockSpec((s, LANE), lambda b, h: (b, COL_K + h)),
            pl.BlockSpec((s, LANE), lambda b, h: (b, COL_V + h)),
            pl.BlockSpec((None, 4, DA_QK_DIM), lambda b, h: (layer, 0, 0)),
            pl.BlockSpec((None, 1, DA_V_DIM), lambda b, h: (layer, 0, 0)),
        ],
        out_specs=pl.BlockSpec((s, LANE), lambda b, h: (b, h)),
        out_shape=jax.ShapeDtypeStruct((tok.t_ctx, DA_WIDTH), BF16),
        compiler_params=_params(("parallel", "parallel")),
        name="diff_attn_ctx",
    )(p, p, p, da_lambda, da_norm)


def _attn_lat_kernel(q_ref, k_ref, v_ref, ck_ref, cv_ref, cos_ref, sin_ref, lamv_ref, g_ref, o_ref,
                     kall_ref, vall_ref, *, lam_init, tq, s_lat):
    qi = pl.program_id(2)

    @pl.when(qi == 0)
    def _():
        def fill(c, carry):
            r = pl.multiple_of(c * ROPE_CHUNK, ROPE_CHUNK)
            rows = pl.ds(r, ROPE_CHUNK)
            kall_ref[rows, :] = _rope(k_ref[rows, :], cos_ref[rows, :], sin_ref[rows, :]).astype(BF16)
            vall_ref[rows, :] = v_ref[rows, :].astype(BF16)
            return carry

        lax.fori_loop(0, s_lat // ROPE_CHUNK, fill, 0)
        past = ck_ref.shape[0]
        kall_ref[pl.ds(s_lat, past), :] = ck_ref[...].astype(BF16)
        vall_ref[pl.ds(s_lat, past), :] = cv_ref[...].astype(BF16)

    rows = pl.ds(pl.multiple_of(qi * tq, tq), tq)
    q = _rope(q_ref[...], cos_ref[rows, :], sin_ref[rows, :])
    lam = _lambda(lamv_ref, lam_init)
    o_ref[...] = _diff_attn(q, kall_ref[...], vall_ref[...], lam, g_ref[...], lam_init)


def _attn_lat(tok, p, cache_k, cache_v, cos, sin, da_lambda, da_norm, layer, lam_init):
    tq = 128
    s = tok.s_lat
    past = cache_k.shape[2]
    q0 = tok.t_ctx // tq
    kv0 = tok.t_ctx // s
    nq = s // tq
    return pl.pallas_call(
        functools.partial(_attn_lat_kernel, lam_init=lam_init, tq=tq, s_lat=s),
        grid=(tok.n_lat_seq, DA_HEADS, nq),
        in_specs=[
            pl.BlockSpec((tq, LANE), lambda b, h, i: (q0 + b * nq + i, COL_Q + h)),
            pl.BlockSpec((s, LANE), lambda b, h, i: (kv0 + b, COL_K + h)),
            pl.BlockSpec((s, LANE), lambda b, h, i: (kv0 + b, COL_V + h)),
            pl.BlockSpec((None, None, past, LANE), lambda b, h, i: (b, layer, 0, h)),
            pl.BlockSpec((None, None, past, LANE), lambda b, h, i: (b, layer, 0, h)),
            pl.BlockSpec((s, LANE), lambda b, h, i: (0, 0)),
            pl.BlockSpec((s, LANE), lambda b, h, i: (0, 0)),
            pl.BlockSpec((None, 4, DA_QK_DIM), lambda b, h, i: (layer, 0, 0)),
            pl.BlockSpec((None, 1, DA_V_DIM), lambda b, h, i: (layer, 0, 0)),
        ],
        out_specs=pl.BlockSpec((tq, LANE), lambda b, h, i: (b * nq + i, h)),
        out_shape=jax.ShapeDtypeStruct((tok.t_lat, DA_WIDTH), BF16),
        scratch_shapes=[pltpu.VMEM((s + past, LANE), BF16), pltpu.VMEM((s + past, LANE), BF16)],
        compiler_params=_params(("parallel", "parallel", "arbitrary")),
        name="diff_attn_lat",
    )(p, p, p, cache_k, cache_v, cos, sin, da_lambda, da_norm)


RG_PAD = SUBLANE
RG_CHUNK = 128


def _scan_tile(a, u, hprev, reverse):
    row = lax.broadcasted_iota(jnp.int32, a.shape, 0)
    for k in (1, 2, 4):
        if reverse:
            keep = row < SUBLANE - k
            shift = SUBLANE - k
        else:
            keep = row >= k
            shift = k
        a_prev = jnp.where(keep, pltpu.roll(a, shift, axis=0), 1.0)
        u_prev = jnp.where(keep, pltpu.roll(u, shift, axis=0), 0.0)
        u = u + a * u_prev
        a = a * a_prev
    return a * hprev + u


def _rglru_kernel(rx_ref, rz_ref, cw_ref, cb_ref, wa_ref, ba_ref, wx_ref, bx_ref, lam_ref, h0_ref,
                  o_ref, last_ref, xpad_ref, a_ref, u_ref, hs_ref, *, seq):
    zeros = jnp.zeros((RG_PAD, RG_WIDTH), F32)
    xpad_ref[pl.ds(0, RG_PAD), :] = zeros
    xpad_ref[pl.ds(seq + RG_PAD, RG_PAD), :] = zeros

    def copy_in(c, carry):
        r = pl.multiple_of(c * RG_CHUNK, RG_CHUNK)
        xpad_ref[pl.ds(r + RG_PAD, RG_CHUNK), :] = rx_ref[pl.ds(r, RG_CHUNK), :]
        return carry

    lax.fori_loop(0, seq // RG_CHUNK, copy_in, 0)

    n_tiles = seq // SUBLANE
    for d in range(2):
        reverse = d == 1
        lam = lam_ref[d:d + 1, :]
        decay = -RG_C * (jnp.maximum(-lam, 0.0) + jnp.log(1.0 + jnp.exp(-jnp.abs(lam))))
        first_tap = RG_PAD if reverse else RG_PAD - (RG_CONV_K - 1)

        def gates(c, carry, d=d, decay=decay, first_tap=first_tap):
            r = pl.multiple_of(c * RG_CHUNK, RG_CHUNK)
            xc = jnp.zeros((RG_CHUNK, RG_WIDTH), F32) + cb_ref[d:d + 1, :]
            win = xpad_ref[pl.ds(r, RG_CHUNK + 2 * RG_PAD), :]
            for j in range(RG_CONV_K):
                xc = xc + cw_ref[d, j:j + 1, :] * win[first_tap + j:first_tap + j + RG_CHUNK, :]
            xb = xc.astype(BF16)
            rg = jax.nn.sigmoid(jnp.dot(xb, wa_ref[d], preferred_element_type=F32) + ba_ref[d:d + 1, :])
            ig = jax.nn.sigmoid(jnp.dot(xb, wx_ref[d], preferred_element_type=F32) + bx_ref[d:d + 1, :])
            log_a = rg * decay
            a_ref[pl.ds(r, RG_CHUNK), :] = jnp.exp(log_a)
            u_ref[pl.ds(r, RG_CHUNK), :] = jnp.sqrt(1.0 - jnp.exp(2.0 * log_a)) * ig * xc
            return carry

        lax.fori_loop(0, seq // RG_CHUNK, gates, 0)

        def scan(t, hprev, d=d, reverse=reverse):
            tile = (n_tiles - 1 - t) if reverse else t
            rows = pl.ds(pl.multiple_of(tile * SUBLANE, SUBLANE), SUBLANE)
            h = _scan_tile(a_ref[rows, :], u_ref[rows, :], hprev, reverse)
            if d == 0:
                hs_ref[rows, :] = h
            else:
                hs_ref[rows, :] = hs_ref[rows, :] + h
            return h[0:1, :] if reverse else h[SUBLANE - 1:SUBLANE, :]

        last = lax.fori_loop(0, n_tiles, scan, h0_ref[d:d + 1, :], unroll=4)
        last_ref[d:d + 1, :] = last

    def gate_out(c, carry):
        rows = pl.ds(pl.multiple_of(c * RG_CHUNK, RG_CHUNK), RG_CHUNK)
        o_ref[rows, :] = (hs_ref[rows, :] * jax.nn.gelu(rz_ref[rows, :])).astype(BF16)
        return carry

    lax.fori_loop(0, seq // RG_CHUNK, gate_out, 0)


def _rglru(tok, p, h0, h0_spec, rg, layer, latent):
    seq = tok.s_lat if latent else tok.s_ctx
    nseq = tok.n_lat_seq if latent else tok.n_ctx_seq
    row0 = tok.t_ctx // seq if latent else 0
    vec2 = pl.BlockSpec((None, 2, RG_WIDTH), lambda i: (layer, 0, 0))
    mat2 = pl.BlockSpec((None, 2, RG_WIDTH, RG_WIDTH), lambda i: (layer, 0, 0, 0))
    return pl.pallas_call(
        functools.partial(_rglru_kernel, seq=seq),
        grid=(nseq,),
        in_specs=[
            pl.BlockSpec((seq, RG_WIDTH), lambda i: (row0 + i, COL_RX)),
            pl.BlockSpec((seq, RG_WIDTH), lambda i: (row0 + i, COL_RZ)),
            pl.BlockSpec((None, 2, RG_CONV_K, RG_WIDTH), lambda i: (layer, 0, 0, 0)),
            vec2, mat2, vec2, mat2, vec2, vec2,
            h0_spec,
        ],
        out_specs=[
            pl.BlockSpec((seq, RG_WIDTH), lambda i: (i, 0)),
            pl.BlockSpec((None, 2, RG_WIDTH), lambda i: (i, 0, 0)),
        ],
        out_shape=[
            jax.ShapeDtypeStruct((nseq * seq, RG_WIDTH), BF16),
            jax.ShapeDtypeStruct((nseq, 2, RG_WIDTH), F32),
        ],
        scratch_shapes=[
            pltpu.VMEM((seq + 2 * RG_PAD, RG_WIDTH), F32),
            pltpu.VMEM((seq, RG_WIDTH), F32),
            pltpu.VMEM((seq, RG_WIDTH), F32),
            pltpu.VMEM((seq, RG_WIDTH), F32),
        ],
        compiler_params=_params(("parallel",)),
        name="rglru_lat" if latent else "rglru_ctx",
    )(p, p, rg["conv_w"], rg["conv_b"], rg["w_a"], rg["b_a"], rg["w_x"], rg["b_x"], rg["lam"], h0)


ROUTE_LANES = LANE


def _route(logits):
    lane = lax.broadcasted_iota(jnp.int32, logits.shape, 1).astype(F32)
    neg = -jnp.inf
    big = float(ROUTE_LANES)
    is_grp = lane < N_GROUPS
    gl = jnp.where(is_grp, logits, neg)
    gmax = jnp.max(gl, axis=-1, keepdims=True)
    grp = jnp.min(jnp.where(gl == gmax, lane, big), axis=-1, keepdims=True)
    gsum = jnp.sum(jnp.where(is_grp, jnp.exp(logits - gmax), 0.0), axis=-1, keepdims=True)
    p_grp = 1.0 / gsum
    lo = N_GROUPS + grp * EXPERTS_PER_GROUP
    in_grp = (lane >= lo) & (lane < lo + EXPERTS_PER_GROUP)
    el = jnp.where(in_grp, logits, neg)
    m1 = jnp.max(el, axis=-1, keepdims=True)
    i1 = jnp.min(jnp.where(el == m1, lane, big), axis=-1, keepdims=True)
    el2 = jnp.where(lane == i1, neg, el)
    m2 = jnp.max(el2, axis=-1, keepdims=True)
    i2 = jnp.min(jnp.where(el2 == m2, lane, big), axis=-1, keepdims=True)
    z = jnp.sum(jnp.where(in_grp, jnp.exp(logits - m1), 0.0), axis=-1, keepdims=True)
    w1 = 1.0 / z
    w2 = jnp.exp(m2 - m1) / z
    wsum = w1 + w2
    g1 = w1 / wsum * p_grp
    g2 = w2 / wsum * p_grp
    experts = jnp.where(lane == 0.0, i1 - N_GROUPS, jnp.where(lane == 1.0, i2 - N_GROUPS, 0.0))
    gates = jnp.where(lane == 0.0, g1, jnp.where(lane == 1.0, g2, 0.0))
    return experts.astype(jnp.int32), gates


def _outproj_kernel(conv_ref, da_ref, rg_ref, x_ref, w_ref, gate1_ref, g2_ref, scale2_ref, shift2_ref,
                    wr_ref, br_ref, xo_ref, h2_ref, ei_ref, eg_ref):
    mixed = jnp.dot(conv_ref[...], w_ref[0:CONV_WIDTH, :], preferred_element_type=F32)
    mixed += jnp.dot(da_ref[...], w_ref[CONV_WIDTH:CONV_WIDTH + DA_WIDTH, :], preferred_element_type=F32)
    mixed += jnp.dot(rg_ref[...], w_ref[CONV_WIDTH + DA_WIDTH:MIX_WIDTH, :], preferred_element_type=F32)
    x = x_ref[...] + gate1_ref[...] * mixed
    xo_ref[...] = x
    h2 = _norm_mod(x, g2_ref[...], scale2_ref[...], shift2_ref[...])
    h2_ref[...] = h2
    logits = jnp.dot(h2, wr_ref[...], preferred_element_type=F32, precision=lax.Precision.HIGHEST) + br_ref[...]
    experts, gates = _route(logits)
    ei_ref[...] = experts
    eg_ref[...] = gates


def _outproj(tok, conv_out, da_out, rg_out, x, w_out_b, norm2, mod5, w_route, b_route, layer):
    tm = 256
    row = lambda width: pl.BlockSpec((tm, width), lambda i: (i, 0))
    return pl.pallas_call(
        _outproj_kernel,
        grid=(tok.t // tm,),
        in_specs=[
            row(CONV_WIDTH), row(DA_WIDTH), row(RG_WIDTH), row(D_MODEL),
            pl.BlockSpec((None, MIX_WIDTH, D_MODEL), lambda i: (layer, 0, 0)),
            tok.mod_spec(layer, 2, tm),
            pl.BlockSpec((None, 1, D_MODEL), lambda i: (layer, 0, 0)),
            tok.mod_spec(layer, 4, tm),
            tok.mod_spec(layer, 3, tm),
            pl.BlockSpec((None, D_MODEL, ROUTE_LANES), lambda i: (layer, 0, 0)),
            pl.BlockSpec((None, 1, ROUTE_LANES), lambda i: (layer, 0, 0)),
        ],
        out_specs=[row(D_MODEL), row(D_MODEL), row(ROUTE_LANES), row(ROUTE_LANES)],
        out_shape=[
            jax.ShapeDtypeStruct((tok.t, D_MODEL), F32),
            jax.ShapeDtypeStruct((tok.t, D_MODEL), F32),
            jax.ShapeDtypeStruct((tok.t, ROUTE_LANES), jnp.int32),
            jax.ShapeDtypeStruct((tok.t, ROUTE_LANES), F32),
        ],
        compiler_params=_params(("parallel",)),
        name="outproj_route",
    )(conv_out, da_out, rg_out, x, w_out_b, mod5, norm2, mod5, mod5, w_route, b_route)


def _dispatch_plan(experts):
    t = experts.shape[0]
    a = t * TOP_K
    n_blocks = a // EXPERT_ROWS + N_EXPERTS
    e_flat = experts.reshape(a)
    tok_flat = jnp.arange(a, dtype=jnp.int32) // TOP_K
    order = jnp.argsort(e_flat, stable=True).astype(jnp.int32)
    onehot = e_flat[:, None] == jnp.arange(N_EXPERTS, dtype=jnp.int32)[None, :]
    running = jnp.cumsum(onehot.astype(jnp.int32), axis=0)
    counts = running[-1]
    rank = jnp.sum(jnp.where(onehot, running, 0), axis=1) - 1
    padded = (counts + EXPERT_ROWS - 1) // EXPERT_ROWS * EXPERT_ROWS
    start = jnp.cumsum(counts) - counts
    ends_p = jnp.cumsum(padded)
    pstart = ends_p - padded
    dest = (pstart[e_flat] + rank).astype(jnp.int32)
    blk_expert = jnp.minimum(
        jnp.searchsorted(ends_p, jnp.arange(n_blocks, dtype=jnp.int32) * EXPERT_ROWS, side="right"),
        N_EXPERTS - 1).astype(jnp.int32)
    rows = jnp.arange(n_blocks * EXPERT_ROWS, dtype=jnp.int32)
    row_e = jnp.repeat(blk_expert, EXPERT_ROWS)
    off = rows - pstart[row_e]
    valid = (off >= 0) & (off < counts[row_e])
    src = jnp.clip(start[row_e] + off, 0, a - 1)
    buf_tok = jnp.where(valid, tok_flat[order[src]], 0).astype(jnp.int32)
    n_used = (ends_p[-1] // EXPERT_ROWS).astype(jnp.int32).reshape(1)
    return blk_expert, n_used, buf_tok, dest


def _expert_kernel(bexp_ref, nused_ref, tok_ref, h2_hbm, wg_ref, wu_ref, wd_ref, y_ref, xbuf, sem):
    b = pl.program_id(0)
    n_used = nused_ref[0]

    def row_copy(blk, r, slot):
        t = tok_ref[blk * EXPERT_ROWS + r]
        return pltpu.make_async_copy(h2_hbm.at[pl.ds(t, 1), :], xbuf.at[slot, pl.ds(r, 1), :], sem.at[slot])

    def issue(blk, slot):
        def body(r, carry):
            row_copy(blk, r, slot).start()
            return carry
        lax.fori_loop(0, EXPERT_ROWS, body, 0, unroll=8)

    def drain(blk, slot):
        def body(r, carry):
            row_copy(blk, r, slot).wait()
            return carry
        lax.fori_loop(0, EXPERT_ROWS, body, 0, unroll=8)

    @pl.when(b == 0)
    def _():
        issue(0, 0)

    @pl.when(b + 1 < n_used)
    def _():
        issue(b + 1, (b + 1) % 2)

    @pl.when(b < n_used)
    def _():
        slot = b % 2
        drain(b, slot)
        x = xbuf[slot].astype(BF16)
        gate = jnp.dot(x, wg_ref[...].astype(BF16), preferred_element_type=F32)
        up = jnp.dot(x, wu_ref[...].astype(BF16), preferred_element_type=F32)
        hid = (gate * jax.nn.sigmoid(gate)) * up
        y_ref[...] = jnp.dot(hid.astype(BF16), wd_ref[...].astype(BF16), preferred_element_type=F32)

    @pl.when(b >= n_used)
    def _():
        y_ref[...] = jnp.zeros_like(y_ref)


def _expert_ffn(h2, blk_expert, n_used, buf_tok, w_gate, w_up, w_down, layer):
    n_blocks = blk_expert.shape[0]
    wspec_in = pl.BlockSpec((None, None, D_MODEL, D_EXPERT), lambda b, bexp, nu, tk: (layer, bexp[b], 0, 0))
    wspec_out = pl.BlockSpec((None, None, D_EXPERT, D_MODEL), lambda b, bexp, nu, tk: (layer, bexp[b], 0, 0))
    return pl.pallas_call(
        _expert_kernel,
        grid_spec=pltpu.PrefetchScalarGridSpec(
            num_scalar_prefetch=3,
            grid=(n_blocks,),
            in_specs=[pl.BlockSpec(memory_space=pl.ANY), wspec_in, wspec_in, wspec_out],
            out_specs=pl.BlockSpec((EXPERT_ROWS, D_MODEL), lambda b, bexp, nu, tk: (b, 0)),
            scratch_shapes=[pltpu.VMEM((2, EXPERT_ROWS, D_MODEL), F32), pltpu.SemaphoreType.DMA((2,))],
        ),
        out_shape=jax.ShapeDtypeStruct((n_blocks * EXPERT_ROWS, D_MODEL), F32),
        compiler_params=_params(("arbitrary",)),
        name="expert_ffn",
    )(blk_expert, n_used, buf_tok, h2, w_gate, w_up, w_down)


def _combine_kernel(dest_ref, yb_hbm, x_ref, eg_ref, gate2_ref, g_ref, *rest, last):
    if last:
        y_ref, ybuf, sem = rest
    else:
        scale_ref, shift_ref, xo_ref, h_ref, ybuf, sem = rest
    i = pl.program_id(0)
    n = pl.num_programs(0)
    rows = COMBINE_ROWS * TOP_K

    def row_copy(tile, r, slot):
        d = dest_ref[tile * rows + r]
        return pltpu.make_async_copy(yb_hbm.at[pl.ds(d, 1), :], ybuf.at[slot, pl.ds(r, 1), :], sem.at[slot])

    def issue(tile, slot):
        def body(r, carry):
            row_copy(tile, r, slot).start()
            return carry
        lax.fori_loop(0, rows, body, 0, unroll=8)

    def drain(tile, slot):
        def body(r, carry):
            row_copy(tile, r, slot).wait()
            return carry
        lax.fori_loop(0, rows, body, 0, unroll=8)

    @pl.when(i == 0)
    def _():
        issue(0, 0)

    @pl.when(i + 1 < n)
    def _():
        issue(i + 1, (i + 1) % 2)

    slot = i % 2
    drain(i, slot)
    y0 = ybuf[slot, pl.ds(0, COMBINE_ROWS), :]
    y1 = ybuf[slot, pl.ds(COMBINE_ROWS, COMBINE_ROWS), :]
    eg = eg_ref[...]
    moe = y0 * eg[:, 0:1] + y1 * eg[:, 1:2]
    x = x_ref[...] + gate2_ref[...] * moe
    if last:
        y_ref[...] = x * lax.rsqrt(jnp.mean(x * x, axis=-1, keepdims=True) + EPS) * g_ref[...]
    else:
        xo_ref[...] = x
        h_ref[...] = _norm_mod(x, g_ref[...], scale_ref[...], shift_ref[...]).astype(BF16)


def _combine(tok, yb, dest_tiles, x, eg, mod5, g, layer, last):
    tm = COMBINE_ROWS
    row = lambda width: pl.BlockSpec((tm, width), lambda i, d: (i, 0))
    in_specs = [pl.BlockSpec(memory_space=pl.ANY), row(D_MODEL), row(ROUTE_LANES), tok.mod_spec(layer, 5, tm)]
    args = [dest_tiles, yb, x, eg, mod5, g]
    if last:
        in_specs += [pl.BlockSpec((1, D_MODEL), lambda i, d: (0, 0))]
        out_specs = row(D_MODEL)
        out_shape = jax.ShapeDtypeStruct((tok.t, D_MODEL), F32)
    else:
        in_specs += [pl.BlockSpec((None, 1, D_MODEL), lambda i, d: (layer + 1, 0, 0)),
                     tok.mod_spec(layer + 1, 1, tm), tok.mod_spec(layer + 1, 0, tm)]
        args += [mod5, mod5]
        out_specs = [row(D_MODEL), row(D_MODEL)]
        out_shape = [jax.ShapeDtypeStruct((tok.t, D_MODEL), F32), jax.ShapeDtypeStruct((tok.t, D_MODEL), BF16)]
    return pl.pallas_call(
        functools.partial(_combine_kernel, last=last),
        grid_spec=pltpu.PrefetchScalarGridSpec(
            num_scalar_prefetch=1,
            grid=(tok.t // tm,),
            in_specs=in_specs,
            out_specs=out_specs,
            scratch_shapes=[pltpu.VMEM((2, tm * TOP_K, D_MODEL), F32), pltpu.SemaphoreType.DMA((2,))],
        ),
        out_shape=out_shape,
        compiler_params=_params(("arbitrary",)),
        name="combine_final" if last else "combine_norm",
    )(*args)


def _rope_tables(n_tokens):
    pairs = DA_QK_DIM // 4
    rows = n_tokens // GRID_W
    row_ids = jnp.repeat(jnp.arange(rows, dtype=F32), GRID_W)
    col_ids = jnp.tile(jnp.arange(GRID_W, dtype=F32), rows)
    inv = 1.0 / (ROPE_BASE ** (jnp.arange(pairs, dtype=F32) / pairs))
    ang = jnp.concatenate([row_ids[:, None] * inv, col_ids[:, None] * inv], axis=-1)
    cos, sin = jnp.cos(ang), jnp.sin(ang)
    reps = LANE // DA_QK_DIM
    cos_full = jnp.tile(jnp.concatenate([cos, cos], axis=-1), (1, reps))
    sin_signed = jnp.tile(jnp.concatenate([-sin, sin], axis=-1), (1, reps))
    return cos_full, sin_signed


def _block_diag(w):
    eye = jnp.eye(RG_HEADS, dtype=w.dtype)
    return jnp.einsum("ldhij,hg->ldhigj", w, eye).reshape(DEPTH, 2, RG_WIDTH, RG_WIDTH)


def kernel(x_prompt, x_sample, c, c_ctx, cache_k, cache_v, state_rglru, w_mod, b_mod, norm1, norm2, w_in, w_out,
           conv_dw_w, conv_dw_b, conv_ln_g, conv_ln_b, da_lambda, da_norm, rg_conv_w, rg_conv_b, rg_w_a, rg_b_a,
           rg_w_x, rg_b_x, rg_lambda, moe_w_grp, moe_b_grp, moe_w_exp, moe_b_exp, moe_w_gate, moe_w_up,
           moe_w_down, final_norm):
    n_ctx_seq, s_ctx, d = x_prompt.shape
    n_lat_seq, s_lat, _ = x_sample.shape
    past = cache_k.shape[2]
    assert d == D_MODEL and w_in.shape == (DEPTH, D_MODEL, IN_WIDTH) and c.shape[0] + 1 <= N_COND
    assert s_lat % s_ctx == 0 and (n_ctx_seq * s_ctx) % s_lat == 0 and s_ctx % 256 == 0
    tok = _Tokens(n_ctx_seq, s_ctx, n_lat_seq, s_lat)

    cond = jnp.concatenate([c_ctx[None, :], c, jnp.zeros((N_COND - 1 - n_lat_seq, d), F32)], axis=0)
    mod = _modulation(cond, w_mod, b_mod)
    mod5 = mod.reshape(DEPTH, N_COND, N_MOD, 1, D_MODEL)

    w_in_b = w_in.astype(BF16)
    w_out_b = w_out.astype(BF16)
    rg = {
        "conv_w": rg_conv_w, "conv_b": rg_conv_b,
        "w_a": _block_diag(rg_w_a).astype(BF16), "b_a": rg_b_a,
        "w_x": _block_diag(rg_w_x).astype(BF16), "b_x": rg_b_x,
        "lam": rg_lambda,
    }
    w_route = jnp.concatenate(
        [moe_w_grp, moe_w_exp, jnp.zeros((DEPTH, d, ROUTE_LANES - N_GROUPS - N_EXPERTS), F32)], axis=-1)
    b_route = jnp.concatenate(
        [moe_b_grp, moe_b_exp, jnp.zeros((DEPTH, ROUTE_LANES - N_GROUPS - N_EXPERTS), F32)], axis=-1)
    b_route = b_route.reshape(DEPTH, 1, ROUTE_LANES)
    norm1_r = norm1.reshape(DEPTH, 1, d)
    norm2_r = norm2.reshape(DEPTH, 1, d)
    dw_b = conv_dw_b.reshape(DEPTH, 1, CONV_WIDTH)
    ln_g = conv_ln_g.reshape(DEPTH, 1, CONV_WIDTH)
    ln_b = conv_ln_b.reshape(DEPTH, 1, CONV_WIDTH)
    da_norm_r = da_norm.reshape(DEPTH, 1, DA_V_DIM)
    cache_k_r = cache_k.reshape(n_lat_seq, DEPTH, past, DA_WIDTH)
    cache_v_r = cache_v.reshape(n_lat_seq, DEPTH, past, DA_WIDTH)
    cos, sin = _rope_tables(s_lat)
    h0_ctx = jnp.zeros((n_ctx_seq, 2, RG_WIDTH), F32)
    h0_ctx_spec = pl.BlockSpec((None, 2, RG_WIDTH), lambda i: (i, 0, 0))

    x = jnp.concatenate([x_prompt.reshape(tok.t_ctx, d), x_sample.reshape(tok.t_lat, d)], axis=0)
    h = _prenorm(tok, x, norm1_r, mod5, 0)
    new_k, new_v, new_h = [], [], []
    for layer in range(DEPTH):
        lam_init = 0.8 - 0.6 * math.exp(-0.3 * layer)
        p = _inproj(tok, h, w_in_b, layer)
        new_k.append(p[:tok.t_ctx, COL_K * LANE:COL_V * LANE].reshape(n_ctx_seq, s_ctx, DA_HEADS, 2 * DA_QK_DIM))
        new_v.append(p[:tok.t_ctx, COL_V * LANE:COL_V * LANE + DA_WIDTH].reshape(n_ctx_seq, s_ctx, DA_HEADS, DA_V_DIM))

        conv_out = jnp.concatenate([
            _conformer_conv(tok, p, conv_dw_w, dw_b, ln_g, ln_b, layer, False),
            _conformer_conv(tok, p, conv_dw_w, dw_b, ln_g, ln_b, layer, True)], axis=0)
        da_out = jnp.concatenate([
            _attn_ctx(tok, p, da_lambda, da_norm_r, layer, lam_init),
            _attn_lat(tok, p, cache_k_r, cache_v_r, cos, sin, da_lambda, da_norm_r, layer, lam_init)], axis=0)
        rg_ctx, last_ctx = _rglru(tok, p, h0_ctx, h0_ctx_spec, rg, layer, False)
        h0_lat_spec = pl.BlockSpec((None, None, 2, RG_WIDTH), lambda i, layer=layer: (i, layer, 0, 0))
        rg_lat, _ = _rglru(tok, p, state_rglru, h0_lat_spec, rg, layer, True)
        rg_out = jnp.concatenate([rg_ctx, rg_lat], axis=0)
        new_h.append(last_ctx)

        x, h2, ei, eg = _outproj(tok, conv_out, da_out, rg_out, x, w_out_b, norm2_r, mod5, w_route, b_route, layer)

        blk_expert, n_used, buf_tok, dest = _dispatch_plan(ei[:, :TOP_K])
        yb = _expert_ffn(h2, blk_expert, n_used, buf_tok, moe_w_gate, moe_w_up, moe_w_down, layer)
        dest_tiles = dest.reshape(tok.t // COMBINE_ROWS, COMBINE_ROWS, TOP_K).transpose(0, 2, 1).reshape(-1)
        last = layer == DEPTH - 1
        if last:
            y = _combine(tok, yb, dest_tiles, x, eg, mod5, final_norm.reshape(1, d), layer, True)
        else:
            x, h = _combine(tok, yb, dest_tiles, x, eg, mod5, norm1_r, layer, False)

    y_prompt = y[:tok.t_ctx].reshape(n_ctx_seq, s_ctx, d)
    y_sample = y[tok.t_ctx:].reshape(n_lat_seq, s_lat, d)
    return (y_prompt, y_sample, jnp.stack(new_k, axis=1), jnp.stack(new_v, axis=1), jnp.stack(new_h, axis=1))
```

```python
import functools
import math

import jax
import jax.numpy as jnp
from jax import lax
from jax.experimental import pallas as pl
from jax.experimental.pallas import tpu as pltpu

F32 = jnp.float32
BF16 = jnp.bfloat16

D_MODEL = 2048
DEPTH = 4
GRID_W = 64
CONV_WIDTH = 512
CONV_K = 31
DA_HEADS = 8
DA_QK_DIM = 64
DA_V_DIM = 128
DA_WIDTH = DA_HEADS * DA_V_DIM
RG_WIDTH = 512
RG_HEADS = 8
RG_HEAD_DIM = RG_WIDTH // RG_HEADS
RG_CONV_K = 4
RG_C = 8.0
IN_WIDTH = 2 * CONV_WIDTH + 3 * DA_WIDTH + 2 * RG_WIDTH
MIX_WIDTH = CONV_WIDTH + DA_WIDTH + RG_WIDTH
N_GROUPS = 4
EXPERTS_PER_GROUP = 8
N_EXPERTS = N_GROUPS * EXPERTS_PER_GROUP
TOP_K = 2
D_EXPERT = 512
ROPE_BASE = 10000.0
EPS = 1e-6
N_COND = 8
N_MOD = 6

LANE = 128
SUBLANE = 8
COL_Q = (2 * CONV_WIDTH) // LANE
COL_K = COL_Q + DA_WIDTH // LANE
COL_V = COL_K + DA_WIDTH // LANE
COL_RX = (2 * CONV_WIDTH + 3 * DA_WIDTH) // RG_WIDTH
COL_RZ = COL_RX + 1

EXPERT_ROWS = 256
COMBINE_ROWS = 128
VMEM_LIMIT = 56 * 1024 * 1024


def _params(sem, vmem=VMEM_LIMIT):
    return pltpu.CompilerParams(dimension_semantics=sem, vmem_limit_bytes=vmem)


def _norm_mod(x, g, scale, shift):
    y = x * lax.rsqrt(jnp.mean(x * x, axis=-1, keepdims=True) + EPS)
    return (y * g) * (1.0 + scale) + shift


class _Tokens:
    def __init__(self, n_ctx_seq, s_ctx, n_lat_seq, s_lat):
        self.n_ctx_seq, self.s_ctx, self.n_lat_seq, self.s_lat = n_ctx_seq, s_ctx, n_lat_seq, s_lat
        self.t_ctx = n_ctx_seq * s_ctx
        self.t_lat = n_lat_seq * s_lat
        self.t = self.t_ctx + self.t_lat

    def cond(self, i, tm):
        n_ctx_tiles = self.t_ctx // tm
        per_seq = self.s_lat // tm
        return jnp.where(i < n_ctx_tiles, 0, 1 + (i - n_ctx_tiles) // per_seq)

    def mod_spec(self, layer, which, tm):
        return pl.BlockSpec((None, None, None, 1, D_MODEL),
                            lambda i, *_: (layer, self.cond(i, tm), which, 0, 0))


def _mod_kernel(c_ref, w_ref, b_ref, o_ref):
    c = c_ref[...]
    s = c * jax.nn.sigmoid(c)
    o_ref[...] = jnp.dot(s.astype(BF16), w_ref[...].astype(BF16), preferred_element_type=F32) + b_ref[...]


def _modulation(cond, w_mod, b_mod):
    tn = 1024
    n = N_MOD * D_MODEL
    return pl.pallas_call(
        _mod_kernel,
        grid=(DEPTH, n // tn),
        in_specs=[
            pl.BlockSpec((N_COND, D_MODEL), lambda l, j: (0, 0)),
            pl.BlockSpec((None, D_MODEL, tn), lambda l, j: (l, 0, j)),
            pl.BlockSpec((None, 1, tn), lambda l, j: (l, 0, j)),
        ],
        out_specs=pl.BlockSpec((None, N_COND, tn), lambda l, j: (l, 0, j)),
        out_shape=jax.ShapeDtypeStruct((DEPTH, N_COND, n), F32),
        compiler_params=_params(("parallel", "parallel")),
        name="modulation",
    )(cond, w_mod, b_mod.reshape(DEPTH, 1, n))


def _prenorm_kernel(x_ref, g_ref, scale_ref, shift_ref, h_ref):
    h_ref[...] = _norm_mod(x_ref[...], g_ref[...], scale_ref[...], shift_ref[...]).astype(BF16)


def _prenorm(tok, x, norm1, mod5, layer):
    tm = 256
    return pl.pallas_call(
        _prenorm_kernel,
        grid=(tok.t // tm,),
        in_specs=[
            pl.BlockSpec((tm, D_MODEL), lambda i: (i, 0)),
            pl.BlockSpec((None, 1, D_MODEL), lambda i: (layer, 0, 0)),
            tok.mod_spec(layer, 1, tm),
            tok.mod_spec(layer, 0, tm),
        ],
        out_specs=pl.BlockSpec((tm, D_MODEL), lambda i: (i, 0)),
        out_shape=jax.ShapeDtypeStruct((tok.t, D_MODEL), BF16),
        compiler_params=_params(("parallel",)),
        name="prenorm",
    )(x, norm1, mod5, mod5)


def _inproj_kernel(h_ref, w_ref, p_ref):
    p_ref[...] = jnp.dot(h_ref[...], w_ref[...], preferred_element_type=F32)


def _inproj(tok, h, w_in_b, layer):
    tm, tn = 512, 1024
    return pl.pallas_call(
        _inproj_kernel,
        grid=(IN_WIDTH // tn, tok.t // tm),
        in_specs=[
            pl.BlockSpec((tm, D_MODEL), lambda j, i: (i, 0)),
            pl.BlockSpec((None, D_MODEL, tn), lambda j, i: (layer, 0, j)),
        ],
        out_specs=pl.BlockSpec((tm, tn), lambda j, i: (i, j)),
        out_shape=jax.ShapeDtypeStruct((tok.t, IN_WIDTH), F32),
        compiler_params=_params(("parallel", "parallel")),
        name="inproj",
    )(h, w_in_b)


CONV_PAD = 16
CONV_CHUNK = 32
GLU_CHUNK = 128


def _conv_kernel(ca_ref, cg_ref, w_ref, b_ref, g_ref, beta_ref, o_ref, upad_ref, *, seq):
    zeros = jnp.zeros((CONV_PAD, CONV_WIDTH), F32)
    upad_ref[pl.ds(0, CONV_PAD), :] = zeros
    upad_ref[pl.ds(seq + CONV_PAD, CONV_PAD), :] = zeros

    def glu(c, carry):
        r = pl.multiple_of(c * GLU_CHUNK, GLU_CHUNK)
        u = ca_ref[pl.ds(r, GLU_CHUNK), :] * jax.nn.sigmoid(cg_ref[pl.ds(r, GLU_CHUNK), :])
        upad_ref[pl.ds(r + CONV_PAD, GLU_CHUNK), :] = u
        return carry

    lax.fori_loop(0, seq // GLU_CHUNK, glu, 0)

    half = CONV_K // 2

    def conv(c, carry):
        r = pl.multiple_of(c * CONV_CHUNK, CONV_CHUNK)
        acc = jnp.zeros((CONV_CHUNK, CONV_WIDTH), F32) + b_ref[...]
        win = upad_ref[pl.ds(r, CONV_CHUNK + 2 * CONV_PAD), :]
        for j in range(CONV_K):
            o = CONV_PAD - half + j
            acc = acc + w_ref[j:j + 1, :] * win[o:o + CONV_CHUNK, :]
        mu = jnp.mean(acc, axis=-1, keepdims=True)
        d = acc - mu
        var = jnp.mean(d * d, axis=-1, keepdims=True)
        un = d * lax.rsqrt(var + EPS) * g_ref[...] + beta_ref[...]
        o_ref[pl.ds(r, CONV_CHUNK), :] = (un * jax.nn.sigmoid(un)).astype(BF16)
        return carry

    lax.fori_loop(0, seq // CONV_CHUNK, conv, 0)


def _conformer_conv(tok, p, dw_w, dw_b, ln_g, ln_b, layer, latent):
    seq = tok.s_lat if latent else tok.s_ctx
    nseq = tok.n_lat_seq if latent else tok.n_ctx_seq
    row0 = tok.t_ctx // seq if latent else 0
    vec = pl.BlockSpec((None, 1, CONV_WIDTH), lambda i: (layer, 0, 0))
    return pl.pallas_call(
        functools.partial(_conv_kernel, seq=seq),
        grid=(nseq,),
        in_specs=[
            pl.BlockSpec((seq, CONV_WIDTH), lambda i: (row0 + i, 0)),
            pl.BlockSpec((seq, CONV_WIDTH), lambda i: (row0 + i, 1)),
            pl.BlockSpec((None, CONV_K, CONV_WIDTH), lambda i: (layer, 0, 0)),
            vec, vec, vec,
        ],
        out_specs=pl.BlockSpec((seq, CONV_WIDTH), lambda i: (i, 0)),
        out_shape=jax.ShapeDtypeStruct((nseq * seq, CONV_WIDTH), BF16),
        scratch_shapes=[pltpu.VMEM((seq + 2 * CONV_PAD, CONV_WIDTH), F32)],
        compiler_params=_params(("parallel",)),
        name="conformer_conv_lat" if latent else "conformer_conv_ctx",
    )(p, p, dw_w, dw_b, ln_g, ln_b)


ATTN_SCALE = DA_QK_DIM ** -0.5
LOG2_E = math.log2(math.e)
ROPE_CHUNK = 256


def _rope(x, cos, sin_signed):
    lane = lax.broadcasted_iota(jnp.int32, x.shape, 1)
    first = (lane % DA_QK_DIM) < (DA_QK_DIM // 2)
    partner = jnp.where(first, pltpu.roll(x, LANE - DA_QK_DIM // 2, axis=1), pltpu.roll(x, DA_QK_DIM // 2, axis=1))
    return x * cos + partner * sin_signed


def _lambda(lamv_ref, lam_init):
    lv = lamv_ref[...]
    a = jnp.sum(lv[0:1, :] * lv[1:2, :], axis=-1, keepdims=True)
    b = jnp.sum(lv[2:3, :] * lv[3:4, :], axis=-1, keepdims=True)
    return jnp.exp(a) - jnp.exp(b) + lam_init


ATTN_ROWS = 128


def _diff_attn(q, kb, vb, lam, g, lam_init, rows=ATTN_ROWS):
    n = q.shape[0] // rows
    outs = [_diff_attn_rows(q[c * rows:(c + 1) * rows], kb, vb, lam, g, lam_init) for c in range(n)]
    return outs[0] if n == 1 else jnp.concatenate(outs, axis=0)


def _diff_attn_rows(q, kb, vb, lam, g, lam_init):
    tq = q.shape[0]
    lane = lax.broadcasted_iota(jnp.int32, q.shape, 1)
    qs = q * (ATTN_SCALE * LOG2_E)
    q1 = jnp.where(lane < DA_QK_DIM, qs, 0.0).astype(BF16)
    q2 = jnp.where(lane >= DA_QK_DIM, qs, 0.0).astype(BF16)
    qq = jnp.concatenate([q1, q2], axis=0)
    s = lax.dot_general(qq, kb, (((1,), (1,)), ((), ())), preferred_element_type=F32)
    m = jnp.max(s, axis=-1, keepdims=True)
    e = jnp.exp2(s - m)
    inv = 1.0 / jnp.sum(e, axis=-1, keepdims=True)
    ov = jnp.dot(e.astype(BF16), vb, preferred_element_type=F32) * inv
    o = ov[:tq] - lam * ov[tq:]
    y = o * lax.rsqrt(jnp.mean(o * o, axis=-1, keepdims=True) + EPS) * g
    return (y * (1.0 - lam_init)).astype(BF16)


def _attn_ctx_kernel(q_ref, k_ref, v_ref, lamv_ref, g_ref, o_ref, ko_ref, vo_ref, *, lam_init):
    lam = _lambda(lamv_ref, lam_init)
    for h in range(DA_HEADS):
        cols = slice(h * LANE, (h + 1) * LANE)
        k = k_ref[:, cols]
        v = v_ref[:, cols]
        ko_ref[:, h, :] = k
        vo_ref[:, h, :] = v
        o_ref[:, cols] = _diff_attn(q_ref[:, cols], k.astype(BF16), v.astype(BF16), lam, g_ref[...], lam_init,
                                    rows=q_ref.shape[0])


def _attn_ctx(tok, p, da_lambda, da_norm, layer, lam_init):
    s = tok.s_ctx
    cache_shape = jax.ShapeDtypeStruct((tok.n_ctx_seq, s, DA_HEADS, DA_V_DIM), F32)
    cache_spec = pl.BlockSpec((None, s, DA_HEADS, DA_V_DIM), lambda b: (b, 0, 0, 0))
    return pl.pallas_call(
        functools.partial(_attn_ctx_kernel, lam_init=lam_init),
        grid=(tok.n_ctx_seq,),
        in_specs=[
            pl.BlockSpec((s, DA_WIDTH), lambda b: (b, COL_Q * LANE // DA_WIDTH)),
            pl.BlockSpec((s, DA_WIDTH), lambda b: (b, COL_K * LANE // DA_WIDTH)),
            pl.BlockSpec((s, DA_WIDTH), lambda b: (b, COL_V * LANE // DA_WIDTH)),
            pl.BlockSpec((None, 4, DA_QK_DIM), lambda b: (layer, 0, 0)),
            pl.BlockSpec((None, 1, DA_V_DIM), lambda b: (layer, 0, 0)),
        ],
        out_specs=[pl.BlockSpec((s, DA_WIDTH), lambda b: (b, 0)), cache_spec, cache_spec],
        out_shape=[jax.ShapeDtypeStruct((tok.t_ctx, DA_WIDTH), BF16), cache_shape, cache_shape],
        compiler_params=_params(("parallel",)),
        name="diff_attn_ctx",
    )(p, p, p, da_lambda, da_norm)


def _attn_lat_kernel(q_ref, k_ref, v_ref, ck_ref, cv_ref, cos_ref, sin_ref, lamv_ref, g_ref, o_ref,
                     kall_ref, vall_ref, *, lam_init, tq, s_lat):
    qi = pl.program_id(2)

    @pl.when(qi == 0)
    def _():
        def fill(c, carry):
            r = pl.multiple_of(c * ROPE_CHUNK, ROPE_CHUNK)
            rows = pl.ds(r, ROPE_CHUNK)
            kall_ref[rows, :] = _rope(k_ref[rows, :], cos_ref[rows, :], sin_ref[rows, :]).astype(BF16)
            vall_ref[rows, :] = v_ref[rows, :].astype(BF16)
            return carry

        lax.fori_loop(0, s_lat // ROPE_CHUNK, fill, 0)
        past = ck_ref.shape[0]
        kall_ref[pl.ds(s_lat, past), :] = ck_ref[...].astype(BF16)
        vall_ref[pl.ds(s_lat, past), :] = cv_ref[...].astype(BF16)

    rows = pl.ds(pl.multiple_of(qi * tq, tq), tq)
    q = _rope(q_ref[...], cos_ref[rows, :], sin_ref[rows, :])
    lam = _lambda(lamv_ref, lam_init)
    o_ref[...] = _diff_attn(q, kall_ref[...], vall_ref[...], lam, g_ref[...], lam_init)


def _attn_lat(tok, p, cache_k, cache_v, cos, sin, da_lambda, da_norm, layer, lam_init):
    tq = 512
    s = tok.s_lat
    past = cache_k.shape[2]
    q0 = tok.t_ctx // tq
    kv0 = tok.t_ctx // s
    nq = s // tq
    return pl.pallas_call(
        functools.partial(_attn_lat_kernel, lam_init=lam_init, tq=tq, s_lat=s),
        grid=(tok.n_lat_seq, DA_HEADS, nq),
        in_specs=[
            pl.BlockSpec((tq, LANE), lambda b, h, i: (q0 + b * nq + i, COL_Q + h)),
            pl.BlockSpec((s, LANE), lambda b, h, i: (kv0 + b, COL_K + h)),
            pl.BlockSpec((s, LANE), lambda b, h, i: (kv0 + b, COL_V + h)),
            pl.BlockSpec((None, None, past, LANE), lambda b, h, i: (b, layer, 0, h)),
            pl.BlockSpec((None, None, past, LANE), lambda b, h, i: (b, layer, 0, h)),
            pl.BlockSpec((s, LANE), lambda b, h, i: (0, 0)),
            pl.BlockSpec((s, LANE), lambda b, h, i: (0, 0)),
            pl.BlockSpec((None, 4, DA_QK_DIM), lambda b, h, i: (layer, 0, 0)),
            pl.BlockSpec((None, 1, DA_V_DIM), lambda b, h, i: (layer, 0, 0)),
        ],
        out_specs=pl.BlockSpec((tq, LANE), lambda b, h, i: (b * nq + i, h)),
        out_shape=jax.ShapeDtypeStruct((tok.t_lat, DA_WIDTH), BF16),
        scratch_shapes=[pltpu.VMEM((s + past, LANE), BF16), pltpu.VMEM((s + past, LANE), BF16)],
        compiler_params=_params(("parallel", "parallel", "arbitrary")),
        name="diff_attn_lat",
    )(p, p, p, cache_k, cache_v, cos, sin, da_lambda, da_norm)


RG_PAD = SUBLANE
RG_CHUNK = 128


def _scan_tile(a, u, hprev, reverse):
    row = lax.broadcasted_iota(jnp.int32, a.shape, 0)
    for k in (1, 2, 4):
        if reverse:
            keep = row < SUBLANE - k
            shift = SUBLANE - k
        else:
            keep = row >= k
            shift = k
        a_prev = jnp.where(keep, pltpu.roll(a, shift, axis=0), 1.0)
        u_prev = jnp.where(keep, pltpu.roll(u, shift, axis=0), 0.0)
        u = u + a * u_prev
        a = a * a_prev
    return a * hprev + u


def _rglru_kernel(rx_ref, rz_ref, cw_ref, cb_ref, wa_ref, ba_ref, wx_ref, bx_ref, lam_ref, h0_ref,
                  o_ref, last_ref, xpad_ref, a_ref, u_ref, hs_ref, *, seq):
    zeros = jnp.zeros((RG_PAD, RG_WIDTH), F32)
    xpad_ref[pl.ds(0, RG_PAD), :] = zeros
    xpad_ref[pl.ds(seq + RG_PAD, RG_PAD), :] = zeros

    def copy_in(c, carry):
        r = pl.multiple_of(c * RG_CHUNK, RG_CHUNK)
        xpad_ref[pl.ds(r + RG_PAD, RG_CHUNK), :] = rx_ref[pl.ds(r, RG_CHUNK), :]
        return carry

    lax.fori_loop(0, seq // RG_CHUNK, copy_in, 0)

    n_tiles = seq // SUBLANE
    for d in range(2):
        reverse = d == 1
        lam = lam_ref[d:d + 1, :]
        decay = -RG_C * (jnp.maximum(-lam, 0.0) + jnp.log(1.0 + jnp.exp(-jnp.abs(lam))))
        first_tap = RG_PAD if reverse else RG_PAD - (RG_CONV_K - 1)

        def gates(c, carry, d=d, decay=decay, first_tap=first_tap):
            r = pl.multiple_of(c * RG_CHUNK, RG_CHUNK)
            xc = jnp.zeros((RG_CHUNK, RG_WIDTH), F32) + cb_ref[d:d + 1, :]
            win = xpad_ref[pl.ds(r, RG_CHUNK + 2 * RG_PAD), :]
            for j in range(RG_CONV_K):
                xc = xc + cw_ref[d, j:j + 1, :] * win[first_tap + j:first_tap + j + RG_CHUNK, :]
            xb = xc.astype(BF16)
            rg = jax.nn.sigmoid(jnp.dot(xb, wa_ref[d], preferred_element_type=F32) + ba_ref[d:d + 1, :])
            ig = jax.nn.sigmoid(jnp.dot(xb, wx_ref[d], preferred_element_type=F32) + bx_ref[d:d + 1, :])
            log_a = rg * decay
            a_ref[pl.ds(r, RG_CHUNK), :] = jnp.exp(log_a)
            u_ref[pl.ds(r, RG_CHUNK), :] = jnp.sqrt(1.0 - jnp.exp(2.0 * log_a)) * ig * xc
            return carry

        lax.fori_loop(0, seq // RG_CHUNK, gates, 0)

        def scan(t, hprev, d=d, reverse=reverse):
            tile = (n_tiles - 1 - t) if reverse else t
            rows = pl.ds(pl.multiple_of(tile * SUBLANE, SUBLANE), SUBLANE)
            h = _scan_tile(a_ref[rows, :], u_ref[rows, :], hprev, reverse)
            if d == 0:
                hs_ref[rows, :] = h
            else:
                hs_ref[rows, :] = hs_ref[rows, :] + h
            return h[0:1, :] if reverse else h[SUBLANE - 1:SUBLANE, :]

        last = lax.fori_loop(0, n_tiles, scan, h0_ref[d:d + 1, :], unroll=4)
        last_ref[d:d + 1, :] = last

    def gate_out(c, carry):
        rows = pl.ds(pl.multiple_of(c * RG_CHUNK, RG_CHUNK), RG_CHUNK)
        o_ref[rows, :] = (hs_ref[rows, :] * jax.nn.gelu(rz_ref[rows, :])).astype(BF16)
        return carry

    lax.fori_loop(0, seq // RG_CHUNK, gate_out, 0)


def _rglru(tok, p, h0, h0_spec, rg, layer, latent):
    seq = tok.s_lat if latent else tok.s_ctx
    nseq = tok.n_lat_seq if latent else tok.n_ctx_seq
    row0 = tok.t_ctx // seq if latent else 0
    vec2 = pl.BlockSpec((None, 2, RG_WIDTH), lambda i: (layer, 0, 0))
    mat2 = pl.BlockSpec((None, 2, RG_WIDTH, RG_WIDTH), lambda i: (layer, 0, 0, 0))
    return pl.pallas_call(
        functools.partial(_rglru_kernel, seq=seq),
        grid=(nseq,),
        in_specs=[
            pl.BlockSpec((seq, RG_WIDTH), lambda i: (row0 + i, COL_RX)),
            pl.BlockSpec((seq, RG_WIDTH), lambda i: (row0 + i, COL_RZ)),
            pl.BlockSpec((None, 2, RG_CONV_K, RG_WIDTH), lambda i: (layer, 0, 0, 0)),
            vec2, mat2, vec2, mat2, vec2, vec2,
            h0_spec,
        ],
        out_specs=[
            pl.BlockSpec((seq, RG_WIDTH), lambda i: (i, 0)),
            pl.BlockSpec((None, 2, RG_WIDTH), lambda i: (i, 0, 0)),
        ],
        out_shape=[
            jax.ShapeDtypeStruct((nseq * seq, RG_WIDTH), BF16),
            jax.ShapeDtypeStruct((nseq, 2, RG_WIDTH), F32),
        ],
        scratch_shapes=[
            pltpu.VMEM((seq + 2 * RG_PAD, RG_WIDTH), F32),
            pltpu.VMEM((seq, RG_WIDTH), F32),
            pltpu.VMEM((seq, RG_WIDTH), F32),
            pltpu.VMEM((seq, RG_WIDTH), F32),
        ],
        compiler_params=_params(("parallel",)),
        name="rglru_lat" if latent else "rglru_ctx",
    )(p, p, rg["conv_w"], rg["conv_b"], rg["w_a"], rg["b_a"], rg["w_x"], rg["b_x"], rg["lam"], h0)


ROUTE_LANES = LANE


def _route(logits):
    lane = lax.broadcasted_iota(jnp.int32, logits.shape, 1).astype(F32)
    neg = -jnp.inf
    big = float(ROUTE_LANES)
    is_grp = lane < N_GROUPS
    gl = jnp.where(is_grp, logits, neg)
    gmax = jnp.max(gl, axis=-1, keepdims=True)
    grp = jnp.min(jnp.where(gl == gmax, lane, big), axis=-1, keepdims=True)
    gsum = jnp.sum(jnp.where(is_grp, jnp.exp(logits - gmax), 0.0), axis=-1, keepdims=True)
    p_grp = 1.0 / gsum
    lo = N_GROUPS + grp * EXPERTS_PER_GROUP
    in_grp = (lane >= lo) & (lane < lo + EXPERTS_PER_GROUP)
    el = jnp.where(in_grp, logits, neg)
    m1 = jnp.max(el, axis=-1, keepdims=True)
    i1 = jnp.min(jnp.where(el == m1, lane, big), axis=-1, keepdims=True)
    el2 = jnp.where(lane == i1, neg, el)
    m2 = jnp.max(el2, axis=-1, keepdims=True)
    i2 = jnp.min(jnp.where(el2 == m2, lane, big), axis=-1, keepdims=True)
    z = jnp.sum(jnp.where(in_grp, jnp.exp(logits - m1), 0.0), axis=-1, keepdims=True)
    w1 = 1.0 / z
    w2 = jnp.exp(m2 - m1) / z
    wsum = w1 + w2
    g1 = w1 / wsum * p_grp
    g2 = w2 / wsum * p_grp
    experts = jnp.where(lane == 0.0, i1 - N_GROUPS, jnp.where(lane == 1.0, i2 - N_GROUPS, 0.0))
    gates = jnp.where(lane == 0.0, g1, jnp.where(lane == 1.0, g2, 0.0))
    return experts.astype(jnp.int32), gates


def _outproj_kernel(conv_c, da_c, rg_c, conv_l, da_l, rg_l, x_ref, w_ref, gate1_ref, g2_ref, scale2_ref,
                    shift2_ref, wr_ref, br_ref, xo_ref, h2_ref, ei_ref, eg_ref, *, n_ctx_tiles):
    is_ctx = pl.program_id(0) < n_ctx_tiles
    conv = jnp.where(is_ctx, conv_c[...], conv_l[...])
    da = jnp.where(is_ctx, da_c[...], da_l[...])
    rg = jnp.where(is_ctx, rg_c[...], rg_l[...])
    mixed = jnp.dot(conv, w_ref[0:CONV_WIDTH, :], preferred_element_type=F32)
    mixed += jnp.dot(da, w_ref[CONV_WIDTH:CONV_WIDTH + DA_WIDTH, :], preferred_element_type=F32)
    mixed += jnp.dot(rg, w_ref[CONV_WIDTH + DA_WIDTH:MIX_WIDTH, :], preferred_element_type=F32)
    x = x_ref[...] + gate1_ref[...] * mixed
    xo_ref[...] = x
    h2 = _norm_mod(x, g2_ref[...], scale2_ref[...], shift2_ref[...])
    h2_ref[...] = h2
    logits = jnp.dot(h2.astype(BF16), wr_ref[...], preferred_element_type=F32) + br_ref[...]
    experts, gates = _route(logits)
    ei_ref[...] = experts
    eg_ref[...] = gates


def _outproj(tok, mix_ctx, mix_lat, x, w_out_b, norm2, mod5, w_route_b, b_route, layer):
    tm = 512
    n_ctx_tiles = tok.t_ctx // tm
    row = lambda width: pl.BlockSpec((tm, width), lambda i: (i, 0))
    ctx_row = lambda width: pl.BlockSpec((tm, width), lambda i: (jnp.minimum(i, n_ctx_tiles - 1), 0))
    lat_row = lambda width: pl.BlockSpec((tm, width), lambda i: (jnp.maximum(i - n_ctx_tiles, 0), 0))
    widths = (CONV_WIDTH, DA_WIDTH, RG_WIDTH)
    return pl.pallas_call(
        functools.partial(_outproj_kernel, n_ctx_tiles=n_ctx_tiles),
        grid=(tok.t // tm,),
        in_specs=[ctx_row(w) for w in widths] + [lat_row(w) for w in widths] + [
            row(D_MODEL),
            pl.BlockSpec((None, MIX_WIDTH, D_MODEL), lambda i: (layer, 0, 0), pipeline_mode=pl.Buffered(1)),
            tok.mod_spec(layer, 2, tm),
            pl.BlockSpec((None, 1, D_MODEL), lambda i: (layer, 0, 0)),
            tok.mod_spec(layer, 4, tm),
            tok.mod_spec(layer, 3, tm),
            pl.BlockSpec((None, D_MODEL, ROUTE_LANES), lambda i: (layer, 0, 0)),
            pl.BlockSpec((None, 1, ROUTE_LANES), lambda i: (layer, 0, 0)),
        ],
        out_specs=[row(D_MODEL), row(D_MODEL), row(ROUTE_LANES), row(ROUTE_LANES)],
        out_shape=[
            jax.ShapeDtypeStruct((tok.t, D_MODEL), F32),
            jax.ShapeDtypeStruct((tok.t, D_MODEL), F32),
            jax.ShapeDtypeStruct((tok.t, ROUTE_LANES), jnp.int32),
            jax.ShapeDtypeStruct((tok.t, ROUTE_LANES), F32),
        ],
        compiler_params=_params(("parallel",)),
        name="outproj_route",
    )(*mix_ctx, *mix_lat, x, w_out_b, mod5, norm2, mod5, mod5, w_route_b, b_route)


def _dispatch_plan(experts):
    t = experts.shape[0]
    a = t * TOP_K
    n_blocks = a // EXPERT_ROWS + N_EXPERTS
    ids = jnp.arange(N_EXPERTS, dtype=jnp.int32)
    e_flat = experts.reshape(a)
    order = jnp.argsort(e_flat, stable=True).astype(jnp.int32)
    pos = jnp.argsort(order).astype(jnp.int32)
    onehot = e_flat[:, None] == ids[None, :]
    counts = jnp.sum(onehot.astype(jnp.int32), axis=0)
    padded = (counts + EXPERT_ROWS - 1) // EXPERT_ROWS * EXPERT_ROWS
    start = jnp.cumsum(counts) - counts
    ends_p = jnp.cumsum(padded)
    shift = ends_p - padded - start
    dest = pos + jnp.sum(jnp.where(onehot, shift[None, :], 0), axis=1)
    blk_start = jnp.arange(n_blocks, dtype=jnp.int32) * EXPERT_ROWS
    blk_expert = jnp.minimum(jnp.sum((blk_start[:, None] >= ends_p[None, :]).astype(jnp.int32), axis=1),
                             N_EXPERTS - 1)
    src_off = blk_start - jnp.sum(jnp.where(blk_expert[:, None] == ids[None, :], shift[None, :], 0), axis=1)
    tok_sorted = order // TOP_K
    n_used = (ends_p[-1] // EXPERT_ROWS).reshape(1)
    return (blk_expert.astype(jnp.int32), n_used.astype(jnp.int32), src_off.astype(jnp.int32),
            tok_sorted.astype(jnp.int32), dest.astype(jnp.int32))


def _expert_kernel(bexp_ref, nused_ref, srcoff_ref, toks_ref, h2_hbm, wg_ref, wu_ref, wd_ref, y_ref, xbuf, sem):
    b = pl.program_id(0)
    n_used = nused_ref[0]
    last = toks_ref.shape[0] - 1

    def row_copy(base, r, slot):
        t = toks_ref[jnp.minimum(base + r, last)]
        return pltpu.make_async_copy(h2_hbm.at[pl.ds(t, 1), :], xbuf.at[slot, pl.ds(r, 1), :], sem.at[slot])

    def issue(blk, slot):
        base = srcoff_ref[blk]
        for r in range(EXPERT_ROWS):
            row_copy(base, r, slot).start(priority=r % 2)

    def drain(blk, slot):
        base = srcoff_ref[blk]

        def body(r, carry):
            row_copy(base, r, slot).wait()
            return carry
        lax.fori_loop(0, EXPERT_ROWS, body, 0, unroll=8)

    @pl.when(b == 0)
    def _():
        issue(0, 0)

    for nxt_slot in range(2):
        @pl.when((b + 1 < n_used) & ((b + 1) % 2 == nxt_slot))
        def _(nxt_slot=nxt_slot):
            issue(b + 1, nxt_slot)

    @pl.when(b < n_used)
    def _():
        slot = b % 2
        drain(b, slot)
        x = xbuf[slot].astype(BF16)
        gate = jnp.dot(x, wg_ref[...].astype(BF16), preferred_element_type=F32)
        up = jnp.dot(x, wu_ref[...].astype(BF16), preferred_element_type=F32)
        hid = (gate * jax.nn.sigmoid(gate)) * up
        y_ref[...] = jnp.dot(hid.astype(BF16), wd_ref[...].astype(BF16), preferred_element_type=F32)

    @pl.when(b >= n_used)
    def _():
        y_ref[...] = jnp.zeros_like(y_ref)


def _expert_ffn(h2, blk_expert, n_used, src_off, tok_sorted, w_gate, w_up, w_down, layer):
    n_blocks = blk_expert.shape[0]
    wspec_in = pl.BlockSpec((None, None, D_MODEL, D_EXPERT), lambda b, bexp, *_: (layer, bexp[b], 0, 0))
    wspec_out = pl.BlockSpec((None, None, D_EXPERT, D_MODEL), lambda b, bexp, *_: (layer, bexp[b], 0, 0))
    return pl.pallas_call(
        _expert_kernel,
        grid_spec=pltpu.PrefetchScalarGridSpec(
            num_scalar_prefetch=4,
            grid=(n_blocks,),
            in_specs=[pl.BlockSpec(memory_space=pl.ANY), wspec_in, wspec_in, wspec_out],
            out_specs=pl.BlockSpec((EXPERT_ROWS, D_MODEL), lambda b, *_: (b, 0)),
            scratch_shapes=[pltpu.VMEM((2, EXPERT_ROWS, D_MODEL), F32), pltpu.SemaphoreType.DMA((2,))],
        ),
        out_shape=jax.ShapeDtypeStruct((n_blocks * EXPERT_ROWS, D_MODEL), F32),
        compiler_params=_params(("arbitrary",)),
        name="expert_ffn",
    )(blk_expert, n_used, src_off, tok_sorted, h2, w_gate, w_up, w_down)


def _combine_kernel(dest_ref, yb_hbm, x_ref, eg_ref, gate2_ref, g_ref, *rest, last):
    if last:
        y_ref, ybuf, sem = rest
    else:
        scale_ref, shift_ref, xo_ref, h_ref, ybuf, sem = rest
    i = pl.program_id(0)
    n = pl.num_programs(0)
    rows = COMBINE_ROWS * TOP_K

    def row_copy(base, r, slot):
        d = dest_ref[base + r]
        return pltpu.make_async_copy(yb_hbm.at[pl.ds(d, 1), :], ybuf.at[slot, pl.ds(r, 1), :], sem.at[slot])

    def issue(tile, slot):
        base = tile * rows
        for r in range(rows):
            row_copy(base, r, slot).start(priority=r % 2)

    def drain(tile, slot):
        base = tile * rows

        def body(r, carry):
            row_copy(base, r, slot).wait()
            return carry
        lax.fori_loop(0, rows, body, 0, unroll=8)

    @pl.when(i == 0)
    def _():
        issue(0, 0)

    for nxt_slot in range(2):
        @pl.when((i + 1 < n) & ((i + 1) % 2 == nxt_slot))
        def _(nxt_slot=nxt_slot):
            issue(i + 1, nxt_slot)

    slot = i % 2
    drain(i, slot)
    y0 = ybuf[slot, pl.ds(0, COMBINE_ROWS), :]
    y1 = ybuf[slot, pl.ds(COMBINE_ROWS, COMBINE_ROWS), :]
    eg = eg_ref[...]
    moe = y0 * eg[:, 0:1] + y1 * eg[:, 1:2]
    x = x_ref[...] + gate2_ref[...] * moe
    if last:
        y_ref[...] = x * lax.rsqrt(jnp.mean(x * x, axis=-1, keepdims=True) + EPS) * g_ref[...]
    else:
        xo_ref[...] = x
        h_ref[...] = _norm_mod(x, g_ref[...], scale_ref[...], shift_ref[...]).astype(BF16)


def _combine(tok, yb, dest_tiles, x, eg, mod5, g, layer, last):
    tm = COMBINE_ROWS
    row = lambda width: pl.BlockSpec((tm, width), lambda i, d: (i, 0))
    in_specs = [pl.BlockSpec(memory_space=pl.ANY), row(D_MODEL), row(ROUTE_LANES), tok.mod_spec(layer, 5, tm)]
    args = [dest_tiles, yb, x, eg, mod5, g]
    if last:
        in_specs += [pl.BlockSpec((1, D_MODEL), lambda i, d: (0, 0))]
        out_specs = row(D_MODEL)
        out_shape = jax.ShapeDtypeStruct((tok.t, D_MODEL), F32)
    else:
        in_specs += [pl.BlockSpec((None, 1, D_MODEL), lambda i, d: (layer + 1, 0, 0)),
                     tok.mod_spec(layer + 1, 1, tm), tok.mod_spec(layer + 1, 0, tm)]
        args += [mod5, mod5]
        out_specs = [row(D_MODEL), row(D_MODEL)]
        out_shape = [jax.ShapeDtypeStruct((tok.t, D_MODEL), F32), jax.ShapeDtypeStruct((tok.t, D_MODEL), BF16)]
    return pl.pallas_call(
        functools.partial(_combine_kernel, last=last),
        grid_spec=pltpu.PrefetchScalarGridSpec(
            num_scalar_prefetch=1,
            grid=(tok.t // tm,),
            in_specs=in_specs,
            out_specs=out_specs,
            scratch_shapes=[pltpu.VMEM((2, tm * TOP_K, D_MODEL), F32), pltpu.SemaphoreType.DMA((2,))],
        ),
        out_shape=out_shape,
        compiler_params=_params(("arbitrary",)),
        name="combine_final" if last else "combine_norm",
    )(*args)


def _rope_tables(n_tokens):
    pairs = DA_QK_DIM // 4
    rows = n_tokens // GRID_W
    row_ids = jnp.repeat(jnp.arange(rows, dtype=F32), GRID_W)
    col_ids = jnp.tile(jnp.arange(GRID_W, dtype=F32), rows)
    inv = 1.0 / (ROPE_BASE ** (jnp.arange(pairs, dtype=F32) / pairs))
    ang = jnp.concatenate([row_ids[:, None] * inv, col_ids[:, None] * inv], axis=-1)
    cos, sin = jnp.cos(ang), jnp.sin(ang)
    reps = LANE // DA_QK_DIM
    cos_full = jnp.tile(jnp.concatenate([cos, cos], axis=-1), (1, reps))
    sin_signed = jnp.tile(jnp.concatenate([-sin, sin], axis=-1), (1, reps))
    return cos_full, sin_signed


def _block_diag(w):
    eye = jnp.eye(RG_HEADS, dtype=w.dtype)
    return jnp.einsum("ldhij,hg->ldhigj", w, eye).reshape(DEPTH, 2, RG_WIDTH, RG_WIDTH)


def kernel(x_prompt, x_sample, c, c_ctx, cache_k, cache_v, state_rglru, w_mod, b_mod, norm1, norm2, w_in, w_out,
           conv_dw_w, conv_dw_b, conv_ln_g, conv_ln_b, da_lambda, da_norm, rg_conv_w, rg_conv_b, rg_w_a, rg_b_a,
           rg_w_x, rg_b_x, rg_lambda, moe_w_grp, moe_b_grp, moe_w_exp, moe_b_exp, moe_w_gate, moe_w_up,
           moe_w_down, final_norm):
    n_ctx_seq, s_ctx, d = x_prompt.shape
    n_lat_seq, s_lat, _ = x_sample.shape
    past = cache_k.shape[2]
    assert d == D_MODEL and w_in.shape == (DEPTH, D_MODEL, IN_WIDTH) and c.shape[0] + 1 <= N_COND
    assert s_lat % s_ctx == 0 and (n_ctx_seq * s_ctx) % s_lat == 0 and s_ctx % 256 == 0 and s_lat % 512 == 0
    tok = _Tokens(n_ctx_seq, s_ctx, n_lat_seq, s_lat)

    cond = jnp.concatenate([c_ctx[None, :], c, jnp.zeros((N_COND - 1 - n_lat_seq, d), F32)], axis=0)
    mod = _modulation(cond, w_mod, b_mod)
    mod5 = mod.reshape(DEPTH, N_COND, N_MOD, 1, D_MODEL)

    w_in_b = w_in.astype(BF16)
    w_out_b = w_out.astype(BF16)
    rg = {
        "conv_w": rg_conv_w, "conv_b": rg_conv_b,
        "w_a": _block_diag(rg_w_a).astype(BF16), "b_a": rg_b_a,
        "w_x": _block_diag(rg_w_x).astype(BF16), "b_x": rg_b_x,
        "lam": rg_lambda,
    }
    w_route = jnp.concatenate(
        [moe_w_grp, moe_w_exp, jnp.zeros((DEPTH, d, ROUTE_LANES - N_GROUPS - N_EXPERTS), F32)], axis=-1)
    b_route = jnp.concatenate(
        [moe_b_grp, moe_b_exp, jnp.zeros((DEPTH, ROUTE_LANES - N_GROUPS - N_EXPERTS), F32)], axis=-1)
    b_route = b_route.reshape(DEPTH, 1, ROUTE_LANES)
    w_route_b = w_route.astype(BF16)
    norm1_r = norm1.reshape(DEPTH, 1, d)
    norm2_r = norm2.reshape(DEPTH, 1, d)
    dw_b = conv_dw_b.reshape(DEPTH, 1, CONV_WIDTH)
    ln_g = conv_ln_g.reshape(DEPTH, 1, CONV_WIDTH)
    ln_b = conv_ln_b.reshape(DEPTH, 1, CONV_WIDTH)
    da_norm_r = da_norm.reshape(DEPTH, 1, DA_V_DIM)
    cache_k_r = cache_k.reshape(n_lat_seq, DEPTH, past, DA_WIDTH)
    cache_v_r = cache_v.reshape(n_lat_seq, DEPTH, past, DA_WIDTH)
    cos, sin = _rope_tables(s_lat)
    h0_ctx = jnp.zeros((n_ctx_seq, 2, RG_WIDTH), F32)
    h0_ctx_spec = pl.BlockSpec((None, 2, RG_WIDTH), lambda i: (i, 0, 0))

    x = jnp.concatenate([x_prompt.reshape(tok.t_ctx, d), x_sample.reshape(tok.t_lat, d)], axis=0)
    h = _prenorm(tok, x, norm1_r, mod5, 0)
    new_k, new_v, new_h = [], [], []
    for layer in range(DEPTH):
        lam_init = 0.8 - 0.6 * math.exp(-0.3 * layer)
        p = _inproj(tok, h, w_in_b, layer)
        conv_ctx = _conformer_conv(tok, p, conv_dw_w, dw_b, ln_g, ln_b, layer, False)
        conv_lat = _conformer_conv(tok, p, conv_dw_w, dw_b, ln_g, ln_b, layer, True)
        da_ctx, k_l, v_l = _attn_ctx(tok, p, da_lambda, da_norm_r, layer, lam_init)
        new_k.append(k_l)
        new_v.append(v_l)
        da_lat = _attn_lat(tok, p, cache_k_r, cache_v_r, cos, sin, da_lambda, da_norm_r, layer, lam_init)
        rg_ctx, last_ctx = _rglru(tok, p, h0_ctx, h0_ctx_spec, rg, layer, False)
        h0_lat_spec = pl.BlockSpec((None, None, 2, RG_WIDTH), lambda i, layer=layer: (i, layer, 0, 0))
        rg_lat, _ = _rglru(tok, p, state_rglru, h0_lat_spec, rg, layer, True)
        new_h.append(last_ctx)

        x, h2, ei, eg = _outproj(tok, (conv_ctx, da_ctx, rg_ctx), (conv_lat, da_lat, rg_lat), x, w_out_b, norm2_r,
                                 mod5, w_route_b, b_route, layer)

        blk_expert, n_used, src_off, tok_sorted, dest = _dispatch_plan(ei[:, :TOP_K])
        yb = _expert_ffn(h2, blk_expert, n_used, src_off, tok_sorted, moe_w_gate, moe_w_up, moe_w_down, layer)
        dest_tiles = dest.reshape(tok.t // COMBINE_ROWS, COMBINE_ROWS, TOP_K).transpose(0, 2, 1).reshape(-1)
        last = layer == DEPTH - 1
        if last:
            y = _combine(tok, yb, dest_tiles, x, eg, mod5, final_norm.reshape(1, d), layer, True)
        else:
            x, h = _combine(tok, yb, dest_tiles, x, eg, mod5, norm1_r, layer, False)

    y_prompt = y[:tok.t_ctx].reshape(n_ctx_seq, s_ctx, d)
    y_sample = y[tok.t_ctx:].reshape(n_lat_seq, s_lat, d)
    return (y_prompt, y_sample, jnp.stack(new_k, axis=1), jnp.stack(new_v, axis=1), jnp.stack(new_h, axis=1))
```

```python
import functools
import math

import jax
import jax.numpy as jnp
from jax import lax
from jax.experimental import pallas as pl
from jax.experimental.pallas import tpu as pltpu

F32 = jnp.float32
BF16 = jnp.bfloat16

D_MODEL = 2048
DEPTH = 4
GRID_W = 64
CONV_WIDTH = 512
CONV_K = 31
DA_HEADS = 8
DA_QK_DIM = 64
DA_V_DIM = 128
DA_WIDTH = DA_HEADS * DA_V_DIM
RG_WIDTH = 512
RG_HEADS = 8
RG_HEAD_DIM = RG_WIDTH // RG_HEADS
RG_CONV_K = 4
RG_C = 8.0
IN_WIDTH = 2 * CONV_WIDTH + 3 * DA_WIDTH + 2 * RG_WIDTH
MIX_WIDTH = CONV_WIDTH + DA_WIDTH + RG_WIDTH
N_GROUPS = 4
EXPERTS_PER_GROUP = 8
N_EXPERTS = N_GROUPS * EXPERTS_PER_GROUP
TOP_K = 2
D_EXPERT = 512
ROPE_BASE = 10000.0
EPS = 1e-6
N_COND = 8
N_MOD = 6

LANE = 128
SUBLANE = 8
COL_Q = (2 * CONV_WIDTH) // LANE
COL_K = COL_Q + DA_WIDTH // LANE
COL_V = COL_K + DA_WIDTH // LANE
COL_RX = (2 * CONV_WIDTH + 3 * DA_WIDTH) // RG_WIDTH
COL_RZ = COL_RX + 1

EXPERT_ROWS = 256
COMBINE_ROWS = 128
VMEM_LIMIT = 56 * 1024 * 1024


def _params(sem, vmem=VMEM_LIMIT):
    return pltpu.CompilerParams(dimension_semantics=sem, vmem_limit_bytes=vmem)


def _norm_mod(x, g, scale, shift):
    y = x * lax.rsqrt(jnp.mean(x * x, axis=-1, keepdims=True) + EPS)
    return (y * g) * (1.0 + scale) + shift


class _Tokens:
    def __init__(self, n_ctx_seq, s_ctx, n_lat_seq, s_lat):
        self.n_ctx_seq, self.s_ctx, self.n_lat_seq, self.s_lat = n_ctx_seq, s_ctx, n_lat_seq, s_lat
        self.t_ctx = n_ctx_seq * s_ctx
        self.t_lat = n_lat_seq * s_lat
        self.t = self.t_ctx + self.t_lat

    def cond(self, i, tm):
        n_ctx_tiles = self.t_ctx // tm
        per_seq = self.s_lat // tm
        return jnp.where(i < n_ctx_tiles, 0, 1 + (i - n_ctx_tiles) // per_seq)

    def mod_spec(self, layer, which, tm):
        return pl.BlockSpec((None, None, None, 1, D_MODEL),
                            lambda i, *_: (layer, self.cond(i, tm), which, 0, 0))


def _mod_kernel(c_ref, w_ref, b_ref, o_ref):
    c = c_ref[...]
    s = c * jax.nn.sigmoid(c)
    o_ref[...] = jnp.dot(s.astype(BF16), w_ref[...].astype(BF16), preferred_element_type=F32) + b_ref[...]


def _modulation(cond, w_mod, b_mod):
    tn = 1024
    n = N_MOD * D_MODEL
    return pl.pallas_call(
        _mod_kernel,
        grid=(DEPTH, n // tn),
        in_specs=[
            pl.BlockSpec((N_COND, D_MODEL), lambda l, j: (0, 0)),
            pl.BlockSpec((None, D_MODEL, tn), lambda l, j: (l, 0, j)),
            pl.BlockSpec((None, 1, tn), lambda l, j: (l, 0, j)),
        ],
        out_specs=pl.BlockSpec((None, N_COND, tn), lambda l, j: (l, 0, j)),
        out_shape=jax.ShapeDtypeStruct((DEPTH, N_COND, n), F32),
        compiler_params=_params(("parallel", "parallel")),
        name="modulation",
    )(cond, w_mod, b_mod.reshape(DEPTH, 1, n))


def _prenorm_kernel(x_ref, g_ref, scale_ref, shift_ref, h_ref):
    h_ref[...] = _norm_mod(x_ref[...], g_ref[...], scale_ref[...], shift_ref[...]).astype(BF16)


def _prenorm(tok, x, norm1, mod5, layer):
    tm = 256
    return pl.pallas_call(
        _prenorm_kernel,
        grid=(tok.t // tm,),
        in_specs=[
            pl.BlockSpec((tm, D_MODEL), lambda i: (i, 0)),
            pl.BlockSpec((None, 1, D_MODEL), lambda i: (layer, 0, 0)),
            tok.mod_spec(layer, 1, tm),
            tok.mod_spec(layer, 0, tm),
        ],
        out_specs=pl.BlockSpec((tm, D_MODEL), lambda i: (i, 0)),
        out_shape=jax.ShapeDtypeStruct((tok.t, D_MODEL), BF16),
        compiler_params=_params(("parallel",)),
        name="prenorm",
    )(x, norm1, mod5, mod5)


def _inproj_kernel(h_ref, w_ref, p_ref, wb_ref):
    @pl.when(pl.program_id(1) == 0)
    def _():
        wb_ref[...] = w_ref[...].astype(BF16)

    p_ref[...] = jnp.dot(h_ref[...], wb_ref[...], preferred_element_type=F32)


def _inproj(tok, h, w_in, layer):
    tm, tn = 512, 1024
    return pl.pallas_call(
        _inproj_kernel,
        grid=(IN_WIDTH // tn, tok.t // tm),
        in_specs=[
            pl.BlockSpec((tm, D_MODEL), lambda j, i: (i, 0)),
            pl.BlockSpec((None, D_MODEL, tn), lambda j, i: (layer, 0, j)),
        ],
        out_specs=pl.BlockSpec((tm, tn), lambda j, i: (i, j)),
        out_shape=jax.ShapeDtypeStruct((tok.t, IN_WIDTH), F32),
        scratch_shapes=[pltpu.VMEM((D_MODEL, tn), BF16)],
        compiler_params=_params(("arbitrary", "arbitrary")),
        name="inproj",
    )(h, w_in)


CONV_PAD = 16
CONV_CHUNK = 64
GLU_CHUNK = 128


def _conv_kernel(ca_ref, cg_ref, w_ref, b_ref, g_ref, beta_ref, o_ref, upad_ref, *, seq):
    zeros = jnp.zeros((CONV_PAD, CONV_WIDTH), F32)
    upad_ref[pl.ds(0, CONV_PAD), :] = zeros
    upad_ref[pl.ds(seq + CONV_PAD, CONV_PAD), :] = zeros

    def glu(c, carry):
        r = pl.multiple_of(c * GLU_CHUNK, GLU_CHUNK)
        u = ca_ref[pl.ds(r, GLU_CHUNK), :] * jax.nn.sigmoid(cg_ref[pl.ds(r, GLU_CHUNK), :])
        upad_ref[pl.ds(r + CONV_PAD, GLU_CHUNK), :] = u
        return carry

    lax.fori_loop(0, seq // GLU_CHUNK, glu, 0)

    half = CONV_K // 2

    def conv(c, carry):
        r = pl.multiple_of(c * CONV_CHUNK, CONV_CHUNK)
        parts = []
        for lt in range(CONV_WIDTH // LANE):
            lanes = slice(lt * LANE, (lt + 1) * LANE)
            win = upad_ref[pl.ds(r, CONV_CHUNK + 2 * CONV_PAD), lanes]
            acc = jnp.zeros((CONV_CHUNK, LANE), F32) + b_ref[:, lanes]
            rows = CONV_CHUNK + 2 * CONV_PAD
            for s in range(SUBLANE):
                shifted = win if s == 0 else pltpu.roll(win, rows - s, axis=0)
                for a in range(2 * CONV_PAD // SUBLANE):
                    j = a * SUBLANE + s - (CONV_PAD - half)
                    if 0 <= j < CONV_K:
                        acc = acc + w_ref[j:j + 1, lanes] * shifted[a * SUBLANE:a * SUBLANE + CONV_CHUNK, :]
            parts.append(acc)
        acc = jnp.concatenate(parts, axis=1)
        mu = jnp.mean(acc, axis=-1, keepdims=True)
        d = acc - mu
        var = jnp.mean(d * d, axis=-1, keepdims=True)
        un = d * lax.rsqrt(var + EPS) * g_ref[...] + beta_ref[...]
        o_ref[pl.ds(r, CONV_CHUNK), :] = (un * jax.nn.sigmoid(un)).astype(BF16)
        return carry

    lax.fori_loop(0, seq // CONV_CHUNK, conv, 0)


def _conformer_conv(tok, p, dw_w, dw_b, ln_g, ln_b, layer, latent):
    seq = tok.s_lat if latent else tok.s_ctx
    nseq = tok.n_lat_seq if latent else tok.n_ctx_seq
    row0 = tok.t_ctx // seq if latent else 0
    vec = pl.BlockSpec((None, 1, CONV_WIDTH), lambda i: (layer, 0, 0))
    return pl.pallas_call(
        functools.partial(_conv_kernel, seq=seq),
        grid=(nseq,),
        in_specs=[
            pl.BlockSpec((seq, CONV_WIDTH), lambda i: (row0 + i, 0)),
            pl.BlockSpec((seq, CONV_WIDTH), lambda i: (row0 + i, 1)),
            pl.BlockSpec((None, CONV_K, CONV_WIDTH), lambda i: (layer, 0, 0)),
            vec, vec, vec,
        ],
        out_specs=pl.BlockSpec((seq, CONV_WIDTH), lambda i: (i, 0)),
        out_shape=jax.ShapeDtypeStruct((nseq * seq, CONV_WIDTH), BF16),
        scratch_shapes=[pltpu.VMEM((seq + 2 * CONV_PAD, CONV_WIDTH), F32)],
        compiler_params=_params(("parallel",)),
        name="conformer_conv_lat" if latent else "conformer_conv_ctx",
    )(p, p, dw_w, dw_b, ln_g, ln_b)


ATTN_SCALE = DA_QK_DIM ** -0.5
LOG2_E = math.log2(math.e)
ROPE_CHUNK = 256


def _rope(x, cos, sin_signed):
    lane = lax.broadcasted_iota(jnp.int32, x.shape, 1)
    first = (lane % DA_QK_DIM) < (DA_QK_DIM // 2)
    partner = jnp.where(first, pltpu.roll(x, LANE - DA_QK_DIM // 2, axis=1), pltpu.roll(x, DA_QK_DIM // 2, axis=1))
    return x * cos + partner * sin_signed


def _lambda(lamv_ref, lam_init):
    lv = lamv_ref[...]
    a = jnp.sum(lv[0:1, :] * lv[1:2, :], axis=-1, keepdims=True)
    b = jnp.sum(lv[2:3, :] * lv[3:4, :], axis=-1, keepdims=True)
    return jnp.exp(a) - jnp.exp(b) + lam_init


ATTN_ROWS = 128


def _diff_attn(q, kb, vb, lam, g, lam_init, rows=ATTN_ROWS):
    n = q.shape[0] // rows
    outs = [_diff_attn_rows(q[c * rows:(c + 1) * rows], kb, vb, lam, g, lam_init) for c in range(n)]
    return outs[0] if n == 1 else jnp.concatenate(outs, axis=0)


def _diff_attn_rows(q, kb, vb, lam, g, lam_init):
    tq = q.shape[0]
    lane = lax.broadcasted_iota(jnp.int32, q.shape, 1)
    qs = q * (ATTN_SCALE * LOG2_E)
    q1 = jnp.where(lane < DA_QK_DIM, qs, 0.0).astype(BF16)
    q2 = jnp.where(lane >= DA_QK_DIM, qs, 0.0).astype(BF16)
    qq = jnp.concatenate([q1, q2], axis=0)
    s = lax.dot_general(qq, kb, (((1,), (1,)), ((), ())), preferred_element_type=F32)
    m = jnp.max(s, axis=-1, keepdims=True)
    e = jnp.exp2(s - m)
    inv = 1.0 / jnp.sum(e, axis=-1, keepdims=True)
    ov = jnp.dot(e.astype(BF16), vb, preferred_element_type=F32) * inv
    o = ov[:tq] - lam * ov[tq:]
    y = o * lax.rsqrt(jnp.mean(o * o, axis=-1, keepdims=True) + EPS) * g
    return (y * (1.0 - lam_init)).astype(BF16)


def _attn_ctx_kernel(q_ref, k_ref, v_ref, lamv_ref, g_ref, o_ref, ko_ref, vo_ref, *, lam_init):
    lam = _lambda(lamv_ref, lam_init)
    for h in range(DA_HEADS):
        cols = slice(h * LANE, (h + 1) * LANE)
        k = k_ref[:, cols]
        v = v_ref[:, cols]
        ko_ref[:, h, :] = k
        vo_ref[:, h, :] = v
        o_ref[:, cols] = _diff_attn(q_ref[:, cols], k.astype(BF16), v.astype(BF16), lam, g_ref[...], lam_init,
                                    rows=q_ref.shape[0])


def _attn_ctx(tok, p, da_lambda, da_norm, layer, lam_init):
    s = tok.s_ctx
    cache_shape = jax.ShapeDtypeStruct((tok.n_ctx_seq, s, DA_HEADS, DA_V_DIM), F32)
    cache_spec = pl.BlockSpec((None, s, DA_HEADS, DA_V_DIM), lambda b: (b, 0, 0, 0))
    return pl.pallas_call(
        functools.partial(_attn_ctx_kernel, lam_init=lam_init),
        grid=(tok.n_ctx_seq,),
        in_specs=[
            pl.BlockSpec((s, DA_WIDTH), lambda b: (b, COL_Q * LANE // DA_WIDTH)),
            pl.BlockSpec((s, DA_WIDTH), lambda b: (b, COL_K * LANE // DA_WIDTH)),
            pl.BlockSpec((s, DA_WIDTH), lambda b: (b, COL_V * LANE // DA_WIDTH)),
            pl.BlockSpec((None, 4, DA_QK_DIM), lambda b: (layer, 0, 0)),
            pl.BlockSpec((None, 1, DA_V_DIM), lambda b: (layer, 0, 0)),
        ],
        out_specs=[pl.BlockSpec((s, DA_WIDTH), lambda b: (b, 0)), cache_spec, cache_spec],
        out_shape=[jax.ShapeDtypeStruct((tok.t_ctx, DA_WIDTH), BF16), cache_shape, cache_shape],
        compiler_params=_params(("parallel",)),
        name="diff_attn_ctx",
    )(p, p, p, da_lambda, da_norm)


def _attn_lat_kernel(q_ref, k_ref, v_ref, ck_ref, cv_ref, cos_ref, sin_ref, lamv_ref, g_ref, o_ref,
                     kall_ref, vall_ref, *, lam_init, tq, s_lat):
    qi = pl.program_id(2)

    @pl.when(qi == 0)
    def _():
        def fill(c, carry):
            r = pl.multiple_of(c * ROPE_CHUNK, ROPE_CHUNK)
            rows = pl.ds(r, ROPE_CHUNK)
            kall_ref[rows, :] = _rope(k_ref[rows, :], cos_ref[rows, :], sin_ref[rows, :]).astype(BF16)
            vall_ref[rows, :] = v_ref[rows, :].astype(BF16)
            return carry

        lax.fori_loop(0, s_lat // ROPE_CHUNK, fill, 0)
        past = ck_ref.shape[0]
        kall_ref[pl.ds(s_lat, past), :] = ck_ref[...].astype(BF16)
        vall_ref[pl.ds(s_lat, past), :] = cv_ref[...].astype(BF16)

    rows = pl.ds(pl.multiple_of(qi * tq, tq), tq)
    q = _rope(q_ref[...], cos_ref[rows, :], sin_ref[rows, :])
    lam = _lambda(lamv_ref, lam_init)
    o_ref[...] = _diff_attn(q, kall_ref[...], vall_ref[...], lam, g_ref[...], lam_init)


def _attn_lat(tok, p, cache_k, cache_v, cos, sin, da_lambda, da_norm, layer, lam_init):
    s = tok.s_lat
    tq = min(1024, s)
    past = cache_k.shape[2]
    q0 = tok.t_ctx // tq
    kv0 = tok.t_ctx // s
    nq = s // tq
    return pl.pallas_call(
        functools.partial(_attn_lat_kernel, lam_init=lam_init, tq=tq, s_lat=s),
        grid=(tok.n_lat_seq, DA_HEADS, nq),
        in_specs=[
            pl.BlockSpec((tq, LANE), lambda b, h, i: (q0 + b * nq + i, COL_Q + h)),
            pl.BlockSpec((s, LANE), lambda b, h, i: (kv0 + b, COL_K + h)),
            pl.BlockSpec((s, LANE), lambda b, h, i: (kv0 + b, COL_V + h)),
            pl.BlockSpec((None, None, past, LANE), lambda b, h, i: (b, layer, 0, h)),
            pl.BlockSpec((None, None, past, LANE), lambda b, h, i: (b, layer, 0, h)),
            pl.BlockSpec((s, LANE), lambda b, h, i: (0, 0)),
            pl.BlockSpec((s, LANE), lambda b, h, i: (0, 0)),
            pl.BlockSpec((None, 4, DA_QK_DIM), lambda b, h, i: (layer, 0, 0)),
            pl.BlockSpec((None, 1, DA_V_DIM), lambda b, h, i: (layer, 0, 0)),
        ],
        out_specs=pl.BlockSpec((tq, LANE), lambda b, h, i: (b * nq + i, h)),
        out_shape=jax.ShapeDtypeStruct((tok.t_lat, DA_WIDTH), BF16),
        scratch_shapes=[pltpu.VMEM((s + past, LANE), BF16), pltpu.VMEM((s + past, LANE), BF16)],
        compiler_params=_params(("parallel", "parallel", "arbitrary")),
        name="diff_attn_lat",
    )(p, p, p, cache_k, cache_v, cos, sin, da_lambda, da_norm)


RG_PAD = SUBLANE
RG_CHUNK = 128


def _scan_tile(a, u, hprev, reverse):
    row = lax.broadcasted_iota(jnp.int32, a.shape, 0)
    for k in (1, 2, 4):
        if reverse:
            keep = row < SUBLANE - k
            shift = SUBLANE - k
        else:
            keep = row >= k
            shift = k
        a_prev = jnp.where(keep, pltpu.roll(a, shift, axis=0), 1.0)
        u_prev = jnp.where(keep, pltpu.roll(u, shift, axis=0), 0.0)
        u = u + a * u_prev
        a = a * a_prev
    return a * hprev + u


def _rglru_kernel(rx_ref, rz_ref, cw_ref, cb_ref, wa_ref, ba_ref, wx_ref, bx_ref, lam_ref, h0_ref,
                  o_ref, last_ref, xpad_ref, a_ref, u_ref, hs_ref, *, seq):
    zeros = jnp.zeros((RG_PAD, RG_WIDTH), F32)
    xpad_ref[pl.ds(0, RG_PAD), :] = zeros
    xpad_ref[pl.ds(seq + RG_PAD, RG_PAD), :] = zeros

    def copy_in(c, carry):
        r = pl.multiple_of(c * RG_CHUNK, RG_CHUNK)
        xpad_ref[pl.ds(r + RG_PAD, RG_CHUNK), :] = rx_ref[pl.ds(r, RG_CHUNK), :]
        return carry

    lax.fori_loop(0, seq // RG_CHUNK, copy_in, 0)

    n_tiles = seq // SUBLANE
    for d in range(2):
        reverse = d == 1
        lam = lam_ref[d:d + 1, :]
        decay = -RG_C * (jnp.maximum(-lam, 0.0) + jnp.log(1.0 + jnp.exp(-jnp.abs(lam))))
        first_tap = RG_PAD if reverse else RG_PAD - (RG_CONV_K - 1)

        def gates(c, carry, d=d, decay=decay, first_tap=first_tap):
            r = pl.multiple_of(c * RG_CHUNK, RG_CHUNK)
            xc = jnp.zeros((RG_CHUNK, RG_WIDTH), F32) + cb_ref[d:d + 1, :]
            win = xpad_ref[pl.ds(r, RG_CHUNK + 2 * RG_PAD), :]
            for j in range(RG_CONV_K):
                xc = xc + cw_ref[d, j:j + 1, :] * win[first_tap + j:first_tap + j + RG_CHUNK, :]
            xb = xc.astype(BF16)
            rg = jax.nn.sigmoid(jnp.dot(xb, wa_ref[d], preferred_element_type=F32) + ba_ref[d:d + 1, :])
            ig = jax.nn.sigmoid(jnp.dot(xb, wx_ref[d], preferred_element_type=F32) + bx_ref[d:d + 1, :])
            log_a = rg * decay
            a = jnp.exp(log_a)
            a_ref[pl.ds(r, RG_CHUNK), :] = a
            u_ref[pl.ds(r, RG_CHUNK), :] = jnp.sqrt(1.0 - a * a) * ig * xc
            return carry

        lax.fori_loop(0, seq // RG_CHUNK, gates, 0)

        def scan(t, hprev, d=d, reverse=reverse):
            tile = (n_tiles - 1 - t) if reverse else t
            rows = pl.ds(pl.multiple_of(tile * SUBLANE, SUBLANE), SUBLANE)
            h = _scan_tile(a_ref[rows, :], u_ref[rows, :], hprev, reverse)
            if d == 0:
                hs_ref[rows, :] = h
            else:
                hs_ref[rows, :] = hs_ref[rows, :] + h
            return h[0:1, :] if reverse else h[SUBLANE - 1:SUBLANE, :]

        last = lax.fori_loop(0, n_tiles, scan, h0_ref[d:d + 1, :], unroll=4)
        last_ref[d:d + 1, :] = last

    def gate_out(c, carry):
        rows = pl.ds(pl.multiple_of(c * RG_CHUNK, RG_CHUNK), RG_CHUNK)
        o_ref[rows, :] = (hs_ref[rows, :] * jax.nn.gelu(rz_ref[rows, :])).astype(BF16)
        return carry

    lax.fori_loop(0, seq // RG_CHUNK, gate_out, 0)


def _rglru(tok, p, h0, h0_spec, rg, layer, latent):
    seq = tok.s_lat if latent else tok.s_ctx
    nseq = tok.n_lat_seq if latent else tok.n_ctx_seq
    row0 = tok.t_ctx // seq if latent else 0
    vec2 = pl.BlockSpec((None, 2, RG_WIDTH), lambda i: (layer, 0, 0))
    mat2 = pl.BlockSpec((None, 2, RG_WIDTH, RG_WIDTH), lambda i: (layer, 0, 0, 0))
    return pl.pallas_call(
        functools.partial(_rglru_kernel, seq=seq),
        grid=(nseq,),
        in_specs=[
            pl.BlockSpec((seq, RG_WIDTH), lambda i: (row0 + i, COL_RX)),
            pl.BlockSpec((seq, RG_WIDTH), lambda i: (row0 + i, COL_RZ)),
            pl.BlockSpec((None, 2, RG_CONV_K, RG_WIDTH), lambda i: (layer, 0, 0, 0)),
            vec2, mat2, vec2, mat2, vec2, vec2,
            h0_spec,
        ],
        out_specs=[
            pl.BlockSpec((seq, RG_WIDTH), lambda i: (i, 0)),
            pl.BlockSpec((None, 2, RG_WIDTH), lambda i: (i, 0, 0)),
        ],
        out_shape=[
            jax.ShapeDtypeStruct((nseq * seq, RG_WIDTH), BF16),
            jax.ShapeDtypeStruct((nseq, 2, RG_WIDTH), F32),
        ],
        scratch_shapes=[
            pltpu.VMEM((seq + 2 * RG_PAD, RG_WIDTH), F32),
            pltpu.VMEM((seq, RG_WIDTH), F32),
            pltpu.VMEM((seq, RG_WIDTH), F32),
            pltpu.VMEM((seq, RG_WIDTH), F32),
        ],
        compiler_params=_params(("parallel",)),
        name="rglru_lat" if latent else "rglru_ctx",
    )(p, p, rg["conv_w"], rg["conv_b"], rg["w_a"], rg["b_a"], rg["w_x"], rg["b_x"], rg["lam"], h0)


ROUTE_LANES = LANE


def _route(logits):
    lane = lax.broadcasted_iota(jnp.int32, logits.shape, 1).astype(F32)
    neg = -jnp.inf
    big = float(ROUTE_LANES)
    is_grp = lane < N_GROUPS
    gl = jnp.where(is_grp, logits, neg)
    gmax = jnp.max(gl, axis=-1, keepdims=True)
    grp = jnp.min(jnp.where(gl == gmax, lane, big), axis=-1, keepdims=True)
    gsum = jnp.sum(jnp.where(is_grp, jnp.exp(logits - gmax), 0.0), axis=-1, keepdims=True)
    p_grp = 1.0 / gsum
    lo = N_GROUPS + grp * EXPERTS_PER_GROUP
    in_grp = (lane >= lo) & (lane < lo + EXPERTS_PER_GROUP)
    el = jnp.where(in_grp, logits, neg)
    m1 = jnp.max(el, axis=-1, keepdims=True)
    i1 = jnp.min(jnp.where(el == m1, lane, big), axis=-1, keepdims=True)
    el2 = jnp.where(lane == i1, neg, el)
    m2 = jnp.max(el2, axis=-1, keepdims=True)
    i2 = jnp.min(jnp.where(el2 == m2, lane, big), axis=-1, keepdims=True)
    z = jnp.sum(jnp.where(in_grp, jnp.exp(logits - m1), 0.0), axis=-1, keepdims=True)
    w1 = 1.0 / z
    w2 = jnp.exp(m2 - m1) / z
    wsum = w1 + w2
    g1 = w1 / wsum * p_grp
    g2 = w2 / wsum * p_grp
    experts = jnp.where(lane == 0.0, i1 - N_GROUPS, jnp.where(lane == 1.0, i2 - N_GROUPS, 0.0))
    gates = jnp.where(lane == 0.0, g1, jnp.where(lane == 1.0, g2, 0.0))
    return experts.astype(jnp.int32), gates


def _outproj_kernel(conv_c, da_c, rg_c, conv_l, da_l, rg_l, x_ref, w_ref, gate1_ref, g2_ref, scale2_ref,
                    shift2_ref, wr_ref, br_ref, xo_ref, h2_ref, ei_ref, eg_ref, *, n_ctx_tiles):
    is_ctx = pl.program_id(0) < n_ctx_tiles
    conv = jnp.where(is_ctx, conv_c[...], conv_l[...])
    da = jnp.where(is_ctx, da_c[...], da_l[...])
    rg = jnp.where(is_ctx, rg_c[...], rg_l[...])
    mixed = jnp.dot(conv, w_ref[0:CONV_WIDTH, :], preferred_element_type=F32)
    mixed += jnp.dot(da, w_ref[CONV_WIDTH:CONV_WIDTH + DA_WIDTH, :], preferred_element_type=F32)
    mixed += jnp.dot(rg, w_ref[CONV_WIDTH + DA_WIDTH:MIX_WIDTH, :], preferred_element_type=F32)
    x = x_ref[...] + gate1_ref[...] * mixed
    xo_ref[...] = x
    h2 = _norm_mod(x, g2_ref[...], scale2_ref[...], shift2_ref[...])
    h2_ref[...] = h2
    logits = jnp.dot(h2.astype(BF16), wr_ref[...], preferred_element_type=F32) + br_ref[...]
    experts, gates = _route(logits)
    ei_ref[...] = experts
    eg_ref[...] = gates


def _outproj(tok, mix_ctx, mix_lat, x, w_out_b, norm2, mod5, w_route_b, b_route, layer):
    tm = 512
    n_ctx_tiles = tok.t_ctx // tm
    row = lambda width: pl.BlockSpec((tm, width), lambda i: (i, 0))
    ctx_row = lambda width: pl.BlockSpec((tm, width), lambda i: (jnp.minimum(i, n_ctx_tiles - 1), 0))
    lat_row = lambda width: pl.BlockSpec((tm, width), lambda i: (jnp.maximum(i - n_ctx_tiles, 0), 0))
    widths = (CONV_WIDTH, DA_WIDTH, RG_WIDTH)
    return pl.pallas_call(
        functools.partial(_outproj_kernel, n_ctx_tiles=n_ctx_tiles),
        grid=(tok.t // tm,),
        in_specs=[ctx_row(w) for w in widths] + [lat_row(w) for w in widths] + [
            row(D_MODEL),
            pl.BlockSpec((None, MIX_WIDTH, D_MODEL), lambda i: (layer, 0, 0), pipeline_mode=pl.Buffered(1)),
            tok.mod_spec(layer, 2, tm),
            pl.BlockSpec((None, 1, D_MODEL), lambda i: (layer, 0, 0)),
            tok.mod_spec(layer, 4, tm),
            tok.mod_spec(layer, 3, tm),
            pl.BlockSpec((None, D_MODEL, ROUTE_LANES), lambda i: (layer, 0, 0)),
            pl.BlockSpec((None, 1, ROUTE_LANES), lambda i: (layer, 0, 0)),
        ],
        out_specs=[row(D_MODEL), row(D_MODEL), row(ROUTE_LANES), row(ROUTE_LANES)],
        out_shape=[
            jax.ShapeDtypeStruct((tok.t, D_MODEL), F32),
            jax.ShapeDtypeStruct((tok.t, D_MODEL), F32),
            jax.ShapeDtypeStruct((tok.t, ROUTE_LANES), jnp.int32),
            jax.ShapeDtypeStruct((tok.t, ROUTE_LANES), F32),
        ],
        compiler_params=_params(("parallel",)),
        name="outproj_route",
    )(*mix_ctx, *mix_lat, x, w_out_b, mod5, norm2, mod5, mod5, w_route_b, b_route)


def _dispatch_plan(experts):
    t = experts.shape[0]
    a = t * TOP_K
    n_blocks = a // EXPERT_ROWS + N_EXPERTS
    ids = jnp.arange(N_EXPERTS, dtype=jnp.int32)
    e_flat = experts.reshape(a)
    order = jnp.argsort(e_flat, stable=True).astype(jnp.int32)
    pos = jnp.argsort(order).astype(jnp.int32)
    onehot = e_flat[:, None] == ids[None, :]
    counts = jnp.sum(onehot.astype(jnp.int32), axis=0)
    padded = (counts + EXPERT_ROWS - 1) // EXPERT_ROWS * EXPERT_ROWS
    start = jnp.cumsum(counts) - counts
    ends_p = jnp.cumsum(padded)
    shift = ends_p - padded - start
    dest = pos + jnp.sum(jnp.where(onehot, shift[None, :], 0), axis=1)
    blk_start = jnp.arange(n_blocks, dtype=jnp.int32) * EXPERT_ROWS
    blk_expert = jnp.minimum(jnp.sum((blk_start[:, None] >= ends_p[None, :]).astype(jnp.int32), axis=1),
                             N_EXPERTS - 1)
    src_off = blk_start - jnp.sum(jnp.where(blk_expert[:, None] == ids[None, :], shift[None, :], 0), axis=1)
    tok_sorted = jnp.concatenate([order // TOP_K, jnp.zeros((EXPERT_ROWS,), jnp.int32)])
    n_used = (ends_p[-1] // EXPERT_ROWS).reshape(1)
    return (blk_expert.astype(jnp.int32), n_used.astype(jnp.int32), src_off.astype(jnp.int32),
            tok_sorted.astype(jnp.int32), dest.astype(jnp.int32))


def _expert_kernel(bexp_ref, nused_ref, srcoff_ref, toks_ref, h2_hbm, wg_ref, wu_ref, wd_ref, y_ref,
                   xbuf, wgb_ref, wub_ref, wdb_ref, sem):
    b = pl.program_id(0)
    n_used = nused_ref[0]
    last_blk = n_used - 1

    def row_copy(base, r, slot):
        t = toks_ref[base + r]
        return pltpu.make_async_copy(h2_hbm.at[pl.ds(t, 1), :], xbuf.at[slot, pl.ds(r, 1), :], sem.at[slot])

    def issue(blk, slot):
        base = srcoff_ref[blk]
        for r in range(EXPERT_ROWS):
            row_copy(base, r, slot).start(priority=r % 2)

    def drain(blk, slot):
        base = srcoff_ref[blk]

        def body(r, carry):
            row_copy(base, r, slot).wait()
            return carry
        lax.fori_loop(0, EXPERT_ROWS, body, 0, unroll=8)

    @pl.when(b == 0)
    def _():
        issue(0, 0)
        issue(jnp.minimum(1, last_blk), 1)

    new_expert = (b == 0) | (bexp_ref[b] != bexp_ref[jnp.maximum(b - 1, 0)])

    @pl.when((b < n_used) & new_expert)
    def _():
        wgb_ref[...] = wg_ref[...].astype(BF16)
        wub_ref[...] = wu_ref[...].astype(BF16)
        wdb_ref[...] = wd_ref[...].astype(BF16)

    for slot in range(2):
        @pl.when((b < n_used) & (b % 2 == slot))
        def _(slot=slot):
            drain(b, slot)
            x = xbuf[slot].astype(BF16)
            gate = jnp.dot(x, wgb_ref[...], preferred_element_type=F32)
            up = jnp.dot(x, wub_ref[...], preferred_element_type=F32)
            hid = (gate * jax.nn.sigmoid(gate)) * up
            y_ref[...] = jnp.dot(hid.astype(BF16), wdb_ref[...], preferred_element_type=F32)
            issue(jnp.minimum(b + 2, last_blk), slot)

    @pl.when(b == last_blk)
    def _():
        drain(last_blk, 0)
        drain(last_blk, 1)

    @pl.when(b >= n_used)
    def _():
        y_ref[...] = jnp.zeros_like(y_ref)


def _expert_ffn(h2, blk_expert, n_used, src_off, tok_sorted, w_gate, w_up, w_down, layer):
    n_blocks = blk_expert.shape[0]
    wspec_in = pl.BlockSpec((None, None, D_MODEL, D_EXPERT), lambda b, bexp, *_: (layer, bexp[b], 0, 0))
    wspec_out = pl.BlockSpec((None, None, D_EXPERT, D_MODEL), lambda b, bexp, *_: (layer, bexp[b], 0, 0))
    return pl.pallas_call(
        _expert_kernel,
        grid_spec=pltpu.PrefetchScalarGridSpec(
            num_scalar_prefetch=4,
            grid=(n_blocks,),
            in_specs=[pl.BlockSpec(memory_space=pl.ANY), wspec_in, wspec_in, wspec_out],
            out_specs=pl.BlockSpec((EXPERT_ROWS, D_MODEL), lambda b, *_: (b, 0)),
            scratch_shapes=[
                pltpu.VMEM((2, EXPERT_ROWS, D_MODEL), F32),
                pltpu.VMEM((D_MODEL, D_EXPERT), BF16),
                pltpu.VMEM((D_MODEL, D_EXPERT), BF16),
                pltpu.VMEM((D_EXPERT, D_MODEL), BF16),
                pltpu.SemaphoreType.DMA((2,)),
            ],
        ),
        out_shape=jax.ShapeDtypeStruct((n_blocks * EXPERT_ROWS, D_MODEL), F32),
        compiler_params=_params(("arbitrary",)),
        name="expert_ffn",
    )(blk_expert, n_used, src_off, tok_sorted, h2, w_gate, w_up, w_down)


def _combine_kernel(dest_ref, yb_hbm, x_ref, eg_ref, gate2_ref, g_ref, *rest, last):
    if last:
        y_ref, ybuf, sem = rest
    else:
        scale_ref, shift_ref, xo_ref, h_ref, ybuf, sem = rest
    i = pl.program_id(0)
    last_tile = pl.num_programs(0) - 1
    rows = COMBINE_ROWS * TOP_K

    def row_copy(base, r, slot):
        d = dest_ref[base + r]
        return pltpu.make_async_copy(yb_hbm.at[pl.ds(d, 1), :], ybuf.at[slot, pl.ds(r, 1), :], sem.at[slot])

    def issue(tile, slot):
        base = tile * rows
        for r in range(rows):
            row_copy(base, r, slot).start(priority=r % 2)

    def drain(tile, slot):
        base = tile * rows

        def body(r, carry):
            row_copy(base, r, slot).wait()
            return carry
        lax.fori_loop(0, rows, body, 0, unroll=8)

    @pl.when(i == 0)
    def _():
        issue(0, 0)
        issue(jnp.minimum(1, last_tile), 1)

    for slot in range(2):
        @pl.when(i % 2 == slot)
        def _(slot=slot):
            drain(i, slot)
            y0 = ybuf[slot, 0:COMBINE_ROWS, :]
            y1 = ybuf[slot, COMBINE_ROWS:rows, :]
            eg = eg_ref[...]
            moe = y0 * eg[:, 0:1] + y1 * eg[:, 1:2]
            x = x_ref[...] + gate2_ref[...] * moe
            if last:
                y_ref[...] = x * lax.rsqrt(jnp.mean(x * x, axis=-1, keepdims=True) + EPS) * g_ref[...]
            else:
                xo_ref[...] = x
                h_ref[...] = _norm_mod(x, g_ref[...], scale_ref[...], shift_ref[...]).astype(BF16)
            issue(jnp.minimum(i + 2, last_tile), slot)

    @pl.when(i == last_tile)
    def _():
        drain(last_tile, 0)
        drain(last_tile, 1)


def _combine(tok, yb, dest_tiles, x, eg, mod5, g, layer, last):
    tm = COMBINE_ROWS
    row = lambda width: pl.BlockSpec((tm, width), lambda i, d: (i, 0))
    in_specs = [pl.BlockSpec(memory_space=pl.ANY), row(D_MODEL), row(ROUTE_LANES), tok.mod_spec(layer, 5, tm)]
    args = [dest_tiles, yb, x, eg, mod5, g]
    if last:
        in_specs += [pl.BlockSpec((1, D_MODEL), lambda i, d: (0, 0))]
        out_specs = row(D_MODEL)
        out_shape = jax.ShapeDtypeStruct((tok.t, D_MODEL), F32)
    else:
        in_specs += [pl.BlockSpec((None, 1, D_MODEL), lambda i, d: (layer + 1, 0, 0)),
                     tok.mod_spec(layer + 1, 1, tm), tok.mod_spec(layer + 1, 0, tm)]
        args += [mod5, mod5]
        out_specs = [row(D_MODEL), row(D_MODEL)]
        out_shape = [jax.ShapeDtypeStruct((tok.t, D_MODEL), F32), jax.ShapeDtypeStruct((tok.t, D_MODEL), BF16)]
    return pl.pallas_call(
        functools.partial(_combine_kernel, last=last),
        grid_spec=pltpu.PrefetchScalarGridSpec(
            num_scalar_prefetch=1,
            grid=(tok.t // tm,),
            in_specs=in_specs,
            out_specs=out_specs,
            scratch_shapes=[pltpu.VMEM((2, tm * TOP_K, D_MODEL), F32), pltpu.SemaphoreType.DMA((2,))],
        ),
        out_shape=out_shape,
        compiler_params=_params(("arbitrary",)),
        name="combine_final" if last else "combine_norm",
    )(*args)


def _rope_tables(n_tokens):
    pairs = DA_QK_DIM // 4
    rows = n_tokens // GRID_W
    row_ids = jnp.repeat(jnp.arange(rows, dtype=F32), GRID_W)
    col_ids = jnp.tile(jnp.arange(GRID_W, dtype=F32), rows)
    inv = 1.0 / (ROPE_BASE ** (jnp.arange(pairs, dtype=F32) / pairs))
    ang = jnp.concatenate([row_ids[:, None] * inv, col_ids[:, None] * inv], axis=-1)
    cos, sin = jnp.cos(ang), jnp.sin(ang)
    reps = LANE // DA_QK_DIM
    cos_full = jnp.tile(jnp.concatenate([cos, cos], axis=-1), (1, reps))
    sin_signed = jnp.tile(jnp.concatenate([-sin, sin], axis=-1), (1, reps))
    return cos_full, sin_signed


def _block_diag(w):
    eye = jnp.eye(RG_HEADS, dtype=w.dtype)
    return jnp.einsum("ldhij,hg->ldhigj", w, eye).reshape(DEPTH, 2, RG_WIDTH, RG_WIDTH)


def kernel(x_prompt, x_sample, c, c_ctx, cache_k, cache_v, state_rglru, w_mod, b_mod, norm1, norm2, w_in, w_out,
           conv_dw_w, conv_dw_b, conv_ln_g, conv_ln_b, da_lambda, da_norm, rg_conv_w, rg_conv_b, rg_w_a, rg_b_a,
           rg_w_x, rg_b_x, rg_lambda, moe_w_grp, moe_b_grp, moe_w_exp, moe_b_exp, moe_w_gate, moe_w_up,
           moe_w_down, final_norm):
    n_ctx_seq, s_ctx, d = x_prompt.shape
    n_lat_seq, s_lat, _ = x_sample.shape
    past = cache_k.shape[2]
    assert d == D_MODEL and w_in.shape == (DEPTH, D_MODEL, IN_WIDTH) and c.shape[0] + 1 <= N_COND
    assert s_lat % s_ctx == 0 and (n_ctx_seq * s_ctx) % s_lat == 0 and s_ctx % 256 == 0 and s_lat % 512 == 0
    tok = _Tokens(n_ctx_seq, s_ctx, n_lat_seq, s_lat)

    cond = jnp.concatenate([c_ctx[None, :], c, jnp.zeros((N_COND - 1 - n_lat_seq, d), F32)], axis=0)
    mod = _modulation(cond, w_mod, b_mod)
    mod5 = mod.reshape(DEPTH, N_COND, N_MOD, 1, D_MODEL)

    w_out_b = w_out.astype(BF16)
    rg = {
        "conv_w": rg_conv_w, "conv_b": rg_conv_b,
        "w_a": _block_diag(rg_w_a).astype(BF16), "b_a": rg_b_a,
        "w_x": _block_diag(rg_w_x).astype(BF16), "b_x": rg_b_x,
        "lam": rg_lambda,
    }
    w_route = jnp.concatenate(
        [moe_w_grp, moe_w_exp, jnp.zeros((DEPTH, d, ROUTE_LANES - N_GROUPS - N_EXPERTS), F32)], axis=-1)
    b_route = jnp.concatenate(
        [moe_b_grp, moe_b_exp, jnp.zeros((DEPTH, ROUTE_LANES - N_GROUPS - N_EXPERTS), F32)], axis=-1)
    b_route = b_route.reshape(DEPTH, 1, ROUTE_LANES)
    w_route_b = w_route.astype(BF16)
    norm1_r = norm1.reshape(DEPTH, 1, d)
    norm2_r = norm2.reshape(DEPTH, 1, d)
    dw_b = conv_dw_b.reshape(DEPTH, 1, CONV_WIDTH)
    ln_g = conv_ln_g.reshape(DEPTH, 1, CONV_WIDTH)
    ln_b = conv_ln_b.reshape(DEPTH, 1, CONV_WIDTH)
    da_norm_r = da_norm.reshape(DEPTH, 1, DA_V_DIM)
    cache_k_r = cache_k.reshape(n_lat_seq, DEPTH, past, DA_WIDTH)
    cache_v_r = cache_v.reshape(n_lat_seq, DEPTH, past, DA_WIDTH)
    cos, sin = _rope_tables(s_lat)
    h0_ctx = jnp.zeros((n_ctx_seq, 2, RG_WIDTH), F32)
    h0_ctx_spec = pl.BlockSpec((None, 2, RG_WIDTH), lambda i: (i, 0, 0))

    x = jnp.concatenate([x_prompt.reshape(tok.t_ctx, d), x_sample.reshape(tok.t_lat, d)], axis=0)
    h = _prenorm(tok, x, norm1_r, mod5, 0)
    new_k, new_v, new_h = [], [], []
    for layer in range(DEPTH):
        lam_init = 0.8 - 0.6 * math.exp(-0.3 * layer)
        p = _inproj(tok, h, w_in, layer)
        conv_ctx = _conformer_conv(tok, p, conv_dw_w, dw_b, ln_g, ln_b, layer, False)
        conv_lat = _conformer_conv(tok, p, conv_dw_w, dw_b, ln_g, ln_b, layer, True)
        da_ctx, k_l, v_l = _attn_ctx(tok, p, da_lambda, da_norm_r, layer, lam_init)
        new_k.append(k_l)
        new_v.append(v_l)
        da_lat = _attn_lat(tok, p, cache_k_r, cache_v_r, cos, sin, da_lambda, da_norm_r, layer, lam_init)
        rg_ctx, last_ctx = _rglru(tok, p, h0_ctx, h0_ctx_spec, rg, layer, False)
        h0_lat_spec = pl.BlockSpec((None, None, 2, RG_WIDTH), lambda i, layer=layer: (i, layer, 0, 0))
        rg_lat, _ = _rglru(tok, p, state_rglru, h0_lat_spec, rg, layer, True)
        new_h.append(last_ctx)

        x, h2, ei, eg = _outproj(tok, (conv_ctx, da_ctx, rg_ctx), (conv_lat, da_lat, rg_lat), x, w_out_b, norm2_r,
                                 mod5, w_route_b, b_route, layer)

        blk_expert, n_used, src_off, tok_sorted, dest = _dispatch_plan(ei[:, :TOP_K])
        yb = _expert_ffn(h2, blk_expert, n_used, src_off, tok_sorted, moe_w_gate, moe_w_up, moe_w_down, layer)
        dest_tiles = dest.reshape(tok.t // COMBINE_ROWS, COMBINE_ROWS, TOP_K).transpose(0, 2, 1).reshape(-1)
        last = layer == DEPTH - 1
        if last:
            y = _combine(tok, yb, dest_tiles, x, eg, mod5, final_norm.reshape(1, d), layer, True)
        else:
            x, h = _combine(tok, yb, dest_tiles, x, eg, mod5, norm1_r, layer, False)

    y_prompt = y[:tok.t_ctx].reshape(n_ctx_seq, s_ctx, d)
    y_sample = y[tok.t_ctx:].reshape(n_lat_seq, s_lat, d)
    return (y_prompt, y_sample, jnp.stack(new_k, axis=1), jnp.stack(new_v, axis=1), jnp.stack(new_h, axis=1))
```

```python
import functools
import math

import jax
import jax.numpy as jnp
from jax import lax
from jax.experimental import pallas as pl
from jax.experimental.pallas import tpu as pltpu

F32 = jnp.float32
BF16 = jnp.bfloat16

D_MODEL = 2048
DEPTH = 4
GRID_W = 64
CONV_WIDTH = 512
CONV_K = 31
DA_HEADS = 8
DA_QK_DIM = 64
DA_V_DIM = 128
DA_WIDTH = DA_HEADS * DA_V_DIM
RG_WIDTH = 512
RG_HEADS = 8
RG_HEAD_DIM = RG_WIDTH // RG_HEADS
RG_CONV_K = 4
RG_C = 8.0
IN_WIDTH = 2 * CONV_WIDTH + 3 * DA_WIDTH + 2 * RG_WIDTH
MIX_WIDTH = CONV_WIDTH + DA_WIDTH + RG_WIDTH
N_GROUPS = 4
EXPERTS_PER_GROUP = 8
N_EXPERTS = N_GROUPS * EXPERTS_PER_GROUP
TOP_K = 2
D_EXPERT = 512
ROPE_BASE = 10000.0
EPS = 1e-6
N_COND = 8
N_MOD = 6

LANE = 128
SUBLANE = 8
COL_Q = (2 * CONV_WIDTH) // LANE
COL_K = COL_Q + DA_WIDTH // LANE
COL_V = COL_K + DA_WIDTH // LANE
COL_RX = (2 * CONV_WIDTH + 3 * DA_WIDTH) // RG_WIDTH
COL_RZ = COL_RX + 1

EXPERT_ROWS = 256
COMBINE_ROWS = 128
VMEM_LIMIT = 56 * 1024 * 1024


def _params(sem, vmem=VMEM_LIMIT):
    return pltpu.CompilerParams(dimension_semantics=sem, vmem_limit_bytes=vmem)


PACKED = D_MODEL // 2
U32 = jnp.uint32


def _pack_rows(v):
    lo = pltpu.bitcast(v[:, :PACKED].astype(BF16).astype(F32), U32)
    hi = pltpu.bitcast(v[:, PACKED:].astype(BF16).astype(F32), U32)
    return hi | (lo >> 16)


def _unpack_rows(w):
    lo = pltpu.bitcast(w << 16, F32)
    hi = pltpu.bitcast(w & jnp.uint32(0xFFFF0000), F32)
    return jnp.concatenate([lo, hi], axis=1)


def _norm_mod(x, g, scale, shift):
    y = x * lax.rsqrt(jnp.mean(x * x, axis=-1, keepdims=True) + EPS)
    return (y * g) * (1.0 + scale) + shift


class _Tokens:
    def __init__(self, n_ctx_seq, s_ctx, n_lat_seq, s_lat):
        self.n_ctx_seq, self.s_ctx, self.n_lat_seq, self.s_lat = n_ctx_seq, s_ctx, n_lat_seq, s_lat
        self.t_ctx = n_ctx_seq * s_ctx
        self.t_lat = n_lat_seq * s_lat
        self.t = self.t_ctx + self.t_lat

    def cond(self, i, tm):
        n_ctx_tiles = self.t_ctx // tm
        per_seq = self.s_lat // tm
        return jnp.where(i < n_ctx_tiles, 0, 1 + (i - n_ctx_tiles) // per_seq)

    def mod_spec(self, layer, which, tm):
        return pl.BlockSpec((None, None, None, 1, D_MODEL),
                            lambda i, *_: (layer, self.cond(i, tm), which, 0, 0))


def _mod_kernel(c_ref, w_ref, b_ref, o_ref):
    c = c_ref[...]
    s = c * jax.nn.sigmoid(c)
    o_ref[...] = jnp.dot(s.astype(BF16), w_ref[...].astype(BF16), preferred_element_type=F32) + b_ref[...]


def _modulation(cond, w_mod, b_mod):
    tn = 1024
    n = N_MOD * D_MODEL
    return pl.pallas_call(
        _mod_kernel,
        grid=(DEPTH, n // tn),
        in_specs=[
            pl.BlockSpec((N_COND, D_MODEL), lambda l, j: (0, 0)),
            pl.BlockSpec((None, D_MODEL, tn), lambda l, j: (l, 0, j)),
            pl.BlockSpec((None, 1, tn), lambda l, j: (l, 0, j)),
        ],
        out_specs=pl.BlockSpec((None, N_COND, tn), lambda l, j: (l, 0, j)),
        out_shape=jax.ShapeDtypeStruct((DEPTH, N_COND, n), F32),
        compiler_params=_params(("parallel", "parallel")),
        name="modulation",
    )(cond, w_mod, b_mod.reshape(DEPTH, 1, n))


def _prenorm_kernel(x_ref, g_ref, scale_ref, shift_ref, h_ref):
    h_ref[...] = _norm_mod(x_ref[...], g_ref[...], scale_ref[...], shift_ref[...]).astype(BF16)


def _prenorm(tok, x, norm1, mod5, layer):
    tm = 256
    return pl.pallas_call(
        _prenorm_kernel,
        grid=(tok.t // tm,),
        in_specs=[
            pl.BlockSpec((tm, D_MODEL), lambda i: (i, 0)),
            pl.BlockSpec((None, 1, D_MODEL), lambda i: (layer, 0, 0)),
            tok.mod_spec(layer, 1, tm),
            tok.mod_spec(layer, 0, tm),
        ],
        out_specs=pl.BlockSpec((tm, D_MODEL), lambda i: (i, 0)),
        out_shape=jax.ShapeDtypeStruct((tok.t, D_MODEL), BF16),
        compiler_params=_params(("parallel",)),
        name="prenorm",
    )(x, norm1, mod5, mod5)


def _inproj_kernel(h_ref, w_ref, p_ref, wb_ref):
    @pl.when(pl.program_id(1) == 0)
    def _():
        wb_ref[...] = w_ref[...].astype(BF16)

    p_ref[...] = jnp.dot(h_ref[...], wb_ref[...], preferred_element_type=F32)


def _inproj(tok, h, w_in, layer):
    tm, tn = (1024 if tok.t % 1024 == 0 else 512), 1024
    return pl.pallas_call(
        _inproj_kernel,
        grid=(IN_WIDTH // tn, tok.t // tm),
        in_specs=[
            pl.BlockSpec((tm, D_MODEL), lambda j, i: (i, 0)),
            pl.BlockSpec((None, D_MODEL, tn), lambda j, i: (layer, 0, j)),
        ],
        out_specs=pl.BlockSpec((tm, tn), lambda j, i: (i, j)),
        out_shape=jax.ShapeDtypeStruct((tok.t, IN_WIDTH), F32),
        scratch_shapes=[pltpu.VMEM((D_MODEL, tn), BF16)],
        compiler_params=_params(("arbitrary", "arbitrary")),
        name="inproj",
    )(h, w_in)


CONV_PAD = 16
CONV_CHUNK = 64
GLU_CHUNK = 128


def _conv_kernel(ca_ref, cg_ref, w_ref, b_ref, g_ref, beta_ref, o_ref, upad_ref, *, seq):
    zeros = jnp.zeros((CONV_PAD, CONV_WIDTH), F32)
    upad_ref[pl.ds(0, CONV_PAD), :] = zeros
    upad_ref[pl.ds(seq + CONV_PAD, CONV_PAD), :] = zeros

    def glu(c, carry):
        r = pl.multiple_of(c * GLU_CHUNK, GLU_CHUNK)
        u = ca_ref[pl.ds(r, GLU_CHUNK), :] * jax.nn.sigmoid(cg_ref[pl.ds(r, GLU_CHUNK), :])
        upad_ref[pl.ds(r + CONV_PAD, GLU_CHUNK), :] = u
        return carry

    lax.fori_loop(0, seq // GLU_CHUNK, glu, 0)

    half = CONV_K // 2

    def conv(c, carry):
        r = pl.multiple_of(c * CONV_CHUNK, CONV_CHUNK)
        parts = []
        for lt in range(CONV_WIDTH // LANE):
            lanes = slice(lt * LANE, (lt + 1) * LANE)
            win = upad_ref[pl.ds(r, CONV_CHUNK + 2 * CONV_PAD), lanes]
            acc = jnp.zeros((CONV_CHUNK, LANE), F32) + b_ref[:, lanes]
            rows = CONV_CHUNK + 2 * CONV_PAD
            for s in range(SUBLANE):
                shifted = win if s == 0 else pltpu.roll(win, rows - s, axis=0)
                for a in range(2 * CONV_PAD // SUBLANE):
                    j = a * SUBLANE + s - (CONV_PAD - half)
                    if 0 <= j < CONV_K:
                        acc = acc + w_ref[j:j + 1, lanes] * shifted[a * SUBLANE:a * SUBLANE + CONV_CHUNK, :]
            parts.append(acc)
        acc = jnp.concatenate(parts, axis=1)
        mu = jnp.mean(acc, axis=-1, keepdims=True)
        d = acc - mu
        var = jnp.mean(d * d, axis=-1, keepdims=True)
        un = d * lax.rsqrt(var + EPS) * g_ref[...] + beta_ref[...]
        o_ref[pl.ds(r, CONV_CHUNK), :] = (un * jax.nn.sigmoid(un)).astype(BF16)
        return carry

    lax.fori_loop(0, seq // CONV_CHUNK, conv, 0)


def _conformer_conv(tok, p, dw_w, dw_b, ln_g, ln_b, layer, latent):
    seq = tok.s_lat if latent else tok.s_ctx
    nseq = tok.n_lat_seq if latent else tok.n_ctx_seq
    row0 = tok.t_ctx // seq if latent else 0
    vec = pl.BlockSpec((None, 1, CONV_WIDTH), lambda i: (layer, 0, 0))
    return pl.pallas_call(
        functools.partial(_conv_kernel, seq=seq),
        grid=(nseq,),
        in_specs=[
            pl.BlockSpec((seq, CONV_WIDTH), lambda i: (row0 + i, 0)),
            pl.BlockSpec((seq, CONV_WIDTH), lambda i: (row0 + i, 1)),
            pl.BlockSpec((None, CONV_K, CONV_WIDTH), lambda i: (layer, 0, 0)),
            vec, vec, vec,
        ],
        out_specs=pl.BlockSpec((seq, CONV_WIDTH), lambda i: (i, 0)),
        out_shape=jax.ShapeDtypeStruct((nseq * seq, CONV_WIDTH), BF16),
        scratch_shapes=[pltpu.VMEM((seq + 2 * CONV_PAD, CONV_WIDTH), F32)],
        compiler_params=_params(("parallel",)),
        name="conformer_conv_lat" if latent else "conformer_conv_ctx",
    )(p, p, dw_w, dw_b, ln_g, ln_b)


ATTN_SCALE = DA_QK_DIM ** -0.5
LOG2_E = math.log2(math.e)
ROPE_CHUNK = 256


def _rope(x, cos, sin_signed):
    lane = lax.broadcasted_iota(jnp.int32, x.shape, 1)
    first = (lane % DA_QK_DIM) < (DA_QK_DIM // 2)
    partner = jnp.where(first, pltpu.roll(x, LANE - DA_QK_DIM // 2, axis=1), pltpu.roll(x, DA_QK_DIM // 2, axis=1))
    return x * cos + partner * sin_signed


def _lambda(lamv_ref, lam_init):
    lv = lamv_ref[...]
    a = jnp.sum(lv[0:1, :] * lv[1:2, :], axis=-1, keepdims=True)
    b = jnp.sum(lv[2:3, :] * lv[3:4, :], axis=-1, keepdims=True)
    return jnp.exp(a) - jnp.exp(b) + lam_init


ATTN_ROWS = 128


def _diff_attn(q, kb, vb, lam, g, lam_init, rows=ATTN_ROWS):
    n = q.shape[0] // rows
    outs = [_diff_attn_rows(q[c * rows:(c + 1) * rows], kb, vb, lam, g, lam_init) for c in range(n)]
    return outs[0] if n == 1 else jnp.concatenate(outs, axis=0)


def _diff_attn_rows(q, kb, vb, lam, g, lam_init):
    tq = q.shape[0]
    lane = lax.broadcasted_iota(jnp.int32, q.shape, 1)
    qs = q * (ATTN_SCALE * LOG2_E)
    q1 = jnp.where(lane < DA_QK_DIM, qs, 0.0).astype(BF16)
    q2 = jnp.where(lane >= DA_QK_DIM, qs, 0.0).astype(BF16)
    qq = jnp.concatenate([q1, q2], axis=0)
    s = lax.dot_general(qq, kb, (((1,), (1,)), ((), ())), preferred_element_type=F32)
    m = jnp.max(s, axis=-1, keepdims=True)
    e = jnp.exp2(s - m)
    inv = 1.0 / jnp.sum(e, axis=-1, keepdims=True)
    ov = jnp.dot(e.astype(BF16), vb, preferred_element_type=F32) * inv
    o = ov[:tq] - lam * ov[tq:]
    y = o * lax.rsqrt(jnp.mean(o * o, axis=-1, keepdims=True) + EPS) * g
    return (y * (1.0 - lam_init)).astype(BF16)


def _attn_ctx_kernel(q_ref, k_ref, v_ref, lamv_ref, g_ref, o_ref, ko_ref, vo_ref, *, lam_init):
    lam = _lambda(lamv_ref, lam_init)
    for h in range(DA_HEADS):
        cols = slice(h * LANE, (h + 1) * LANE)
        k = k_ref[:, cols]
        v = v_ref[:, cols]
        ko_ref[:, h, :] = k
        vo_ref[:, h, :] = v
        o_ref[:, cols] = _diff_attn(q_ref[:, cols], k.astype(BF16), v.astype(BF16), lam, g_ref[...], lam_init,
                                    rows=q_ref.shape[0])


def _attn_ctx(tok, p, da_lambda, da_norm, layer, lam_init):
    s = tok.s_ctx
    cache_shape = jax.ShapeDtypeStruct((tok.n_ctx_seq, s, DA_HEADS, DA_V_DIM), F32)
    cache_spec = pl.BlockSpec((None, s, DA_HEADS, DA_V_DIM), lambda b: (b, 0, 0, 0))
    return pl.pallas_call(
        functools.partial(_attn_ctx_kernel, lam_init=lam_init),
        grid=(tok.n_ctx_seq,),
        in_specs=[
            pl.BlockSpec((s, DA_WIDTH), lambda b: (b, COL_Q * LANE // DA_WIDTH)),
            pl.BlockSpec((s, DA_WIDTH), lambda b: (b, COL_K * LANE // DA_WIDTH)),
            pl.BlockSpec((s, DA_WIDTH), lambda b: (b, COL_V * LANE // DA_WIDTH)),
            pl.BlockSpec((None, 4, DA_QK_DIM), lambda b: (layer, 0, 0)),
            pl.BlockSpec((None, 1, DA_V_DIM), lambda b: (layer, 0, 0)),
        ],
        out_specs=[pl.BlockSpec((s, DA_WIDTH), lambda b: (b, 0)), cache_spec, cache_spec],
        out_shape=[jax.ShapeDtypeStruct((tok.t_ctx, DA_WIDTH), BF16), cache_shape, cache_shape],
        compiler_params=_params(("parallel",)),
        name="diff_attn_ctx",
    )(p, p, p, da_lambda, da_norm)


def _attn_lat_kernel(q_ref, k_ref, v_ref, ck_ref, cv_ref, cos_ref, sin_ref, lamv_ref, g_ref, o_ref,
                     kall_ref, vall_ref, *, lam_init, tq, s_lat):
    qi = pl.program_id(2)

    @pl.when(qi == 0)
    def _():
        def fill(c, carry):
            r = pl.multiple_of(c * ROPE_CHUNK, ROPE_CHUNK)
            rows = pl.ds(r, ROPE_CHUNK)
            kall_ref[rows, :] = _rope(k_ref[rows, :], cos_ref[rows, :], sin_ref[rows, :]).astype(BF16)
            vall_ref[rows, :] = v_ref[rows, :].astype(BF16)
            return carry

        lax.fori_loop(0, s_lat // ROPE_CHUNK, fill, 0)
        past = ck_ref.shape[0]
        kall_ref[pl.ds(s_lat, past), :] = ck_ref[...].astype(BF16)
        vall_ref[pl.ds(s_lat, past), :] = cv_ref[...].astype(BF16)

    rows = pl.ds(pl.multiple_of(qi * tq, tq), tq)
    q = _rope(q_ref[...], cos_ref[rows, :], sin_ref[rows, :])
    lam = _lambda(lamv_ref, lam_init)
    o_ref[...] = _diff_attn(q, kall_ref[...], vall_ref[...], lam, g_ref[...], lam_init)


def _attn_lat(tok, p, cache_k, cache_v, cos, sin, da_lambda, da_norm, layer, lam_init):
    s = tok.s_lat
    tq = min(1024, s)
    past = cache_k.shape[2]
    q0 = tok.t_ctx // tq
    kv0 = tok.t_ctx // s
    nq = s // tq
    return pl.pallas_call(
        functools.partial(_attn_lat_kernel, lam_init=lam_init, tq=tq, s_lat=s),
        grid=(tok.n_lat_seq, DA_HEADS, nq),
        in_specs=[
            pl.BlockSpec((tq, LANE), lambda b, h, i: (q0 + b * nq + i, COL_Q + h)),
            pl.BlockSpec((s, LANE), lambda b, h, i: (kv0 + b, COL_K + h)),
            pl.BlockSpec((s, LANE), lambda b, h, i: (kv0 + b, COL_V + h)),
            pl.BlockSpec((None, None, past, LANE), lambda b, h, i: (b, layer, 0, h)),
            pl.BlockSpec((None, None, past, LANE), lambda b, h, i: (b, layer, 0, h)),
            pl.BlockSpec((s, LANE), lambda b, h, i: (0, 0)),
            pl.BlockSpec((s, LANE), lambda b, h, i: (0, 0)),
            pl.BlockSpec((None, 4, DA_QK_DIM), lambda b, h, i: (layer, 0, 0)),
            pl.BlockSpec((None, 1, DA_V_DIM), lambda b, h, i: (layer, 0, 0)),
        ],
        out_specs=pl.BlockSpec((tq, LANE), lambda b, h, i: (b * nq + i, h)),
        out_shape=jax.ShapeDtypeStruct((tok.t_lat, DA_WIDTH), BF16),
        scratch_shapes=[pltpu.VMEM((s + past, LANE), BF16), pltpu.VMEM((s + past, LANE), BF16)],
        compiler_params=_params(("parallel", "parallel", "arbitrary")),
        name="diff_attn_lat",
    )(p, p, p, cache_k, cache_v, cos, sin, da_lambda, da_norm)


RG_PAD = SUBLANE
RG_CHUNK = 128


def _scan_tile(a, u, hprev, reverse):
    row = lax.broadcasted_iota(jnp.int32, a.shape, 0)
    for k in (1, 2, 4):
        if reverse:
            keep = row < SUBLANE - k
            shift = SUBLANE - k
        else:
            keep = row >= k
            shift = k
        a_prev = jnp.where(keep, pltpu.roll(a, shift, axis=0), 1.0)
        u_prev = jnp.where(keep, pltpu.roll(u, shift, axis=0), 0.0)
        u = u + a * u_prev
        a = a * a_prev
    return a * hprev + u


def _rglru_kernel(rx_ref, rz_ref, cw_ref, cb_ref, wa_ref, ba_ref, wx_ref, bx_ref, lam_ref, h0_ref,
                  o_ref, last_ref, xpad_ref, a_ref, u_ref, hs_ref, *, seq):
    zeros = jnp.zeros((RG_PAD, RG_WIDTH), F32)
    xpad_ref[pl.ds(0, RG_PAD), :] = zeros
    xpad_ref[pl.ds(seq + RG_PAD, RG_PAD), :] = zeros

    def copy_in(c, carry):
        r = pl.multiple_of(c * RG_CHUNK, RG_CHUNK)
        xpad_ref[pl.ds(r + RG_PAD, RG_CHUNK), :] = rx_ref[pl.ds(r, RG_CHUNK), :]
        return carry

    lax.fori_loop(0, seq // RG_CHUNK, copy_in, 0)

    n_tiles = seq // SUBLANE
    for d in range(2):
        reverse = d == 1
        lam = lam_ref[d:d + 1, :]
        decay = -RG_C * (jnp.maximum(-lam, 0.0) + jnp.log(1.0 + jnp.exp(-jnp.abs(lam))))
        first_tap = RG_PAD if reverse else RG_PAD - (RG_CONV_K - 1)

        def gates(c, carry, d=d, decay=decay, first_tap=first_tap):
            r = pl.multiple_of(c * RG_CHUNK, RG_CHUNK)
            xc = jnp.zeros((RG_CHUNK, RG_WIDTH), F32) + cb_ref[d:d + 1, :]
            win = xpad_ref[pl.ds(r, RG_CHUNK + 2 * RG_PAD), :]
            for j in range(RG_CONV_K):
                xc = xc + cw_ref[d, j:j + 1, :] * win[first_tap + j:first_tap + j + RG_CHUNK, :]
            xb = xc.astype(BF16)
            rg = jax.nn.sigmoid(jnp.dot(xb, wa_ref[d], preferred_element_type=F32) + ba_ref[d:d + 1, :])
            ig = jax.nn.sigmoid(jnp.dot(xb, wx_ref[d], preferred_element_type=F32) + bx_ref[d:d + 1, :])
            log_a = rg * decay
            a = jnp.exp(log_a)
            a_ref[pl.ds(r, RG_CHUNK), :] = a
            u_ref[pl.ds(r, RG_CHUNK), :] = jnp.sqrt(1.0 - a * a) * ig * xc
            return carry

        lax.fori_loop(0, seq // RG_CHUNK, gates, 0)

        def scan(t, hprev, d=d, reverse=reverse):
            tile = (n_tiles - 1 - t) if reverse else t
            rows = pl.ds(pl.multiple_of(tile * SUBLANE, SUBLANE), SUBLANE)
            h = _scan_tile(a_ref[rows, :], u_ref[rows, :], hprev, reverse)
            if d == 0:
                hs_ref[rows, :] = h
            else:
                hs_ref[rows, :] = hs_ref[rows, :] + h
            return h[0:1, :] if reverse else h[SUBLANE - 1:SUBLANE, :]

        last = lax.fori_loop(0, n_tiles, scan, h0_ref[d:d + 1, :], unroll=4)
        last_ref[d:d + 1, :] = last

    def gate_out(c, carry):
        rows = pl.ds(pl.multiple_of(c * RG_CHUNK, RG_CHUNK), RG_CHUNK)
        o_ref[rows, :] = (hs_ref[rows, :] * jax.nn.gelu(rz_ref[rows, :])).astype(BF16)
        return carry

    lax.fori_loop(0, seq // RG_CHUNK, gate_out, 0)


def _rglru(tok, p, h0, h0_spec, rg, layer, latent):
    seq = tok.s_lat if latent else tok.s_ctx
    nseq = tok.n_lat_seq if latent else tok.n_ctx_seq
    row0 = tok.t_ctx // seq if latent else 0
    vec2 = pl.BlockSpec((None, 2, RG_WIDTH), lambda i: (layer, 0, 0))
    mat2 = pl.BlockSpec((None, 2, RG_WIDTH, RG_WIDTH), lambda i: (layer, 0, 0, 0))
    return pl.pallas_call(
        functools.partial(_rglru_kernel, seq=seq),
        grid=(nseq,),
        in_specs=[
            pl.BlockSpec((seq, RG_WIDTH), lambda i: (row0 + i, COL_RX)),
            pl.BlockSpec((seq, RG_WIDTH), lambda i: (row0 + i, COL_RZ)),
            pl.BlockSpec((None, 2, RG_CONV_K, RG_WIDTH), lambda i: (layer, 0, 0, 0)),
            vec2, mat2, vec2, mat2, vec2, vec2,
            h0_spec,
        ],
        out_specs=[
            pl.BlockSpec((seq, RG_WIDTH), lambda i: (i, 0)),
            pl.BlockSpec((None, 2, RG_WIDTH), lambda i: (i, 0, 0)),
        ],
        out_shape=[
            jax.ShapeDtypeStruct((nseq * seq, RG_WIDTH), BF16),
            jax.ShapeDtypeStruct((nseq, 2, RG_WIDTH), F32),
        ],
        scratch_shapes=[
            pltpu.VMEM((seq + 2 * RG_PAD, RG_WIDTH), F32),
            pltpu.VMEM((seq, RG_WIDTH), F32),
            pltpu.VMEM((seq, RG_WIDTH), F32),
            pltpu.VMEM((seq, RG_WIDTH), F32),
        ],
        compiler_params=_params(("parallel",)),
        name="rglru_lat" if latent else "rglru_ctx",
    )(p, p, rg["conv_w"], rg["conv_b"], rg["w_a"], rg["b_a"], rg["w_x"], rg["b_x"], rg["lam"], h0)


ROUTE_LANES = LANE


def _route(logits):
    lane = lax.broadcasted_iota(jnp.int32, logits.shape, 1).astype(F32)
    neg = -jnp.inf
    big = float(ROUTE_LANES)
    is_grp = lane < N_GROUPS
    gl = jnp.where(is_grp, logits, neg)
    gmax = jnp.max(gl, axis=-1, keepdims=True)
    grp = jnp.min(jnp.where(gl == gmax, lane, big), axis=-1, keepdims=True)
    gsum = jnp.sum(jnp.where(is_grp, jnp.exp(logits - gmax), 0.0), axis=-1, keepdims=True)
    p_grp = 1.0 / gsum
    lo = N_GROUPS + grp * EXPERTS_PER_GROUP
    in_grp = (lane >= lo) & (lane < lo + EXPERTS_PER_GROUP)
    el = jnp.where(in_grp, logits, neg)
    m1 = jnp.max(el, axis=-1, keepdims=True)
    i1 = jnp.min(jnp.where(el == m1, lane, big), axis=-1, keepdims=True)
    el2 = jnp.where(lane == i1, neg, el)
    m2 = jnp.max(el2, axis=-1, keepdims=True)
    i2 = jnp.min(jnp.where(el2 == m2, lane, big), axis=-1, keepdims=True)
    z = jnp.sum(jnp.where(in_grp, jnp.exp(logits - m1), 0.0), axis=-1, keepdims=True)
    w1 = 1.0 / z
    w2 = jnp.exp(m2 - m1) / z
    wsum = w1 + w2
    g1 = w1 / wsum * p_grp
    g2 = w2 / wsum * p_grp
    experts = jnp.where(lane == 0.0, i1 - N_GROUPS, jnp.where(lane == 1.0, i2 - N_GROUPS, 0.0))
    gates = jnp.where(lane == 0.0, g1, jnp.where(lane == 1.0, g2, 0.0))
    return experts.astype(jnp.int32), gates


def _outproj_kernel(conv_c, da_c, rg_c, conv_l, da_l, rg_l, x_ref, w_ref, gate1_ref, g2_ref, scale2_ref,
                    shift2_ref, wr_ref, br_ref, xo_ref, h2_ref, ei_ref, eg_ref, *, n_ctx_tiles):
    is_ctx = pl.program_id(0) < n_ctx_tiles
    conv = jnp.where(is_ctx, conv_c[...], conv_l[...])
    da = jnp.where(is_ctx, da_c[...], da_l[...])
    rg = jnp.where(is_ctx, rg_c[...], rg_l[...])
    mixed = jnp.dot(conv, w_ref[0:CONV_WIDTH, :], preferred_element_type=F32)
    mixed += jnp.dot(da, w_ref[CONV_WIDTH:CONV_WIDTH + DA_WIDTH, :], preferred_element_type=F32)
    mixed += jnp.dot(rg, w_ref[CONV_WIDTH + DA_WIDTH:MIX_WIDTH, :], preferred_element_type=F32)
    x = x_ref[...] + gate1_ref[...] * mixed
    xo_ref[...] = x
    h2 = _norm_mod(x, g2_ref[...], scale2_ref[...], shift2_ref[...])
    h2_ref[...] = _pack_rows(h2)
    logits = jnp.dot(h2.astype(BF16), wr_ref[...], preferred_element_type=F32) + br_ref[...]
    experts, gates = _route(logits)
    ei_ref[...] = experts
    eg_ref[...] = gates


def _outproj(tok, mix_ctx, mix_lat, x, w_out_b, norm2, mod5, w_route_b, b_route, layer):
    tm = 512
    n_ctx_tiles = tok.t_ctx // tm
    row = lambda width: pl.BlockSpec((tm, width), lambda i: (i, 0))
    ctx_row = lambda width: pl.BlockSpec((tm, width), lambda i: (jnp.minimum(i, n_ctx_tiles - 1), 0))
    lat_row = lambda width: pl.BlockSpec((tm, width), lambda i: (jnp.maximum(i - n_ctx_tiles, 0), 0))
    widths = (CONV_WIDTH, DA_WIDTH, RG_WIDTH)
    return pl.pallas_call(
        functools.partial(_outproj_kernel, n_ctx_tiles=n_ctx_tiles),
        grid=(tok.t // tm,),
        in_specs=[ctx_row(w) for w in widths] + [lat_row(w) for w in widths] + [
            row(D_MODEL),
            pl.BlockSpec((None, MIX_WIDTH, D_MODEL), lambda i: (layer, 0, 0), pipeline_mode=pl.Buffered(1)),
            tok.mod_spec(layer, 2, tm),
            pl.BlockSpec((None, 1, D_MODEL), lambda i: (layer, 0, 0)),
            tok.mod_spec(layer, 4, tm),
            tok.mod_spec(layer, 3, tm),
            pl.BlockSpec((None, D_MODEL, ROUTE_LANES), lambda i: (layer, 0, 0)),
            pl.BlockSpec((None, 1, ROUTE_LANES), lambda i: (layer, 0, 0)),
        ],
        out_specs=[row(D_MODEL), row(PACKED), row(ROUTE_LANES), row(ROUTE_LANES)],
        out_shape=[
            jax.ShapeDtypeStruct((tok.t, D_MODEL), F32),
            jax.ShapeDtypeStruct((tok.t, PACKED), U32),
            jax.ShapeDtypeStruct((tok.t, ROUTE_LANES), jnp.int32),
            jax.ShapeDtypeStruct((tok.t, ROUTE_LANES), F32),
        ],
        compiler_params=_params(("parallel",)),
        name="outproj_route",
    )(*mix_ctx, *mix_lat, x, w_out_b, mod5, norm2, mod5, mod5, w_route_b, b_route)


def _dispatch_plan(experts):
    t = experts.shape[0]
    a = t * TOP_K
    n_blocks = a // EXPERT_ROWS + N_EXPERTS
    ids = jnp.arange(N_EXPERTS, dtype=jnp.int32)
    e_flat = experts.reshape(a)
    order = jnp.argsort(e_flat, stable=True).astype(jnp.int32)
    pos = jnp.argsort(order).astype(jnp.int32)
    onehot = e_flat[:, None] == ids[None, :]
    counts = jnp.sum(onehot.astype(jnp.int32), axis=0)
    padded = (counts + EXPERT_ROWS - 1) // EXPERT_ROWS * EXPERT_ROWS
    start = jnp.cumsum(counts) - counts
    ends_p = jnp.cumsum(padded)
    shift = ends_p - padded - start
    dest = pos + jnp.sum(jnp.where(onehot, shift[None, :], 0), axis=1)
    blk_start = jnp.arange(n_blocks, dtype=jnp.int32) * EXPERT_ROWS
    blk_expert = jnp.minimum(jnp.sum((blk_start[:, None] >= ends_p[None, :]).astype(jnp.int32), axis=1),
                             N_EXPERTS - 1)
    src_off = blk_start - jnp.sum(jnp.where(blk_expert[:, None] == ids[None, :], shift[None, :], 0), axis=1)
    tok_sorted = jnp.concatenate([order // TOP_K, jnp.zeros((EXPERT_ROWS,), jnp.int32)])
    n_used = (ends_p[-1] // EXPERT_ROWS).reshape(1)
    return (blk_expert.astype(jnp.int32), n_used.astype(jnp.int32), src_off.astype(jnp.int32),
            tok_sorted.astype(jnp.int32), dest.astype(jnp.int32))


def _expert_kernel(bexp_ref, nused_ref, srcoff_ref, toks_ref, h2_hbm, wg_ref, wu_ref, wd_ref, y_ref,
                   xbuf, wgb_ref, wub_ref, wdb_ref, sem):
    b = pl.program_id(0)
    n_used = nused_ref[0]
    last_blk = n_used - 1

    def row_copy(base, r, slot):
        t = toks_ref[base + r]
        return pltpu.make_async_copy(h2_hbm.at[pl.ds(t, 1), :], xbuf.at[slot, pl.ds(r, 1), :], sem.at[slot])

    def issue(blk, slot):
        base = srcoff_ref[blk]
        for r in range(EXPERT_ROWS):
            row_copy(base, r, slot).start(priority=r % 2)

    def drain(blk, slot):
        base = srcoff_ref[blk]

        def body(r, carry):
            row_copy(base, r, slot).wait()
            return carry
        lax.fori_loop(0, EXPERT_ROWS, body, 0, unroll=8)

    @pl.when(b == 0)
    def _():
        issue(0, 0)
        issue(jnp.minimum(1, last_blk), 1)

    new_expert = (b == 0) | (bexp_ref[b] != bexp_ref[jnp.maximum(b - 1, 0)])

    @pl.when((b < n_used) & new_expert)
    def _():
        wgb_ref[...] = wg_ref[...].astype(BF16)
        wub_ref[...] = wu_ref[...].astype(BF16)
        wdb_ref[...] = wd_ref[...].astype(BF16)

    for slot in range(2):
        @pl.when((b < n_used) & (b % 2 == slot))
        def _(slot=slot):
            drain(b, slot)
            x = _unpack_rows(xbuf[slot]).astype(BF16)
            gate = jnp.dot(x, wgb_ref[...], preferred_element_type=F32)
            up = jnp.dot(x, wub_ref[...], preferred_element_type=F32)
            hid = (gate * jax.nn.sigmoid(gate)) * up
            y_ref[...] = _pack_rows(jnp.dot(hid.astype(BF16), wdb_ref[...], preferred_element_type=F32))
            issue(jnp.minimum(b + 2, last_blk), slot)

    @pl.when(b == last_blk)
    def _():
        drain(last_blk, 0)
        drain(last_blk, 1)

    @pl.when(b >= n_used)
    def _():
        y_ref[...] = jnp.zeros_like(y_ref)


def _expert_ffn(h2, blk_expert, n_used, src_off, tok_sorted, w_gate, w_up, w_down, layer):
    n_blocks = blk_expert.shape[0]
    wspec_in = pl.BlockSpec((None, None, D_MODEL, D_EXPERT), lambda b, bexp, *_: (layer, bexp[b], 0, 0))
    wspec_out = pl.BlockSpec((None, None, D_EXPERT, D_MODEL), lambda b, bexp, *_: (layer, bexp[b], 0, 0))
    return pl.pallas_call(
        _expert_kernel,
        grid_spec=pltpu.PrefetchScalarGridSpec(
            num_scalar_prefetch=4,
            grid=(n_blocks,),
            in_specs=[pl.BlockSpec(memory_space=pl.ANY), wspec_in, wspec_in, wspec_out],
            out_specs=pl.BlockSpec((EXPERT_ROWS, PACKED), lambda b, *_: (b, 0)),
            scratch_shapes=[
                pltpu.VMEM((2, EXPERT_ROWS, PACKED), U32),
                pltpu.VMEM((D_MODEL, D_EXPERT), BF16),
                pltpu.VMEM((D_MODEL, D_EXPERT), BF16),
                pltpu.VMEM((D_EXPERT, D_MODEL), BF16),
                pltpu.SemaphoreType.DMA((2,)),
            ],
        ),
        out_shape=jax.ShapeDtypeStruct((n_blocks * EXPERT_ROWS, PACKED), U32),
        compiler_params=_params(("arbitrary",)),
        name="expert_ffn",
    )(blk_expert, n_used, src_off, tok_sorted, h2, w_gate, w_up, w_down)


def _combine_kernel(dest_ref, yb_hbm, x_ref, eg_ref, gate2_ref, g_ref, *rest, last, n_ctx_tiles):
    if last:
        yc_ref, yl_ref, ybuf, sem = rest
    else:
        scale_ref, shift_ref, xo_ref, h_ref, ybuf, sem = rest
    i = pl.program_id(0)
    last_tile = pl.num_programs(0) - 1
    rows = COMBINE_ROWS * TOP_K

    def row_copy(base, r, slot):
        d = dest_ref[base + r]
        return pltpu.make_async_copy(yb_hbm.at[pl.ds(d, 1), :], ybuf.at[slot, pl.ds(r, 1), :], sem.at[slot])

    def issue(tile, slot):
        base = tile * rows
        for r in range(rows):
            row_copy(base, r, slot).start(priority=r % 2)

    def drain(tile, slot):
        base = tile * rows

        def body(r, carry):
            row_copy(base, r, slot).wait()
            return carry
        lax.fori_loop(0, rows, body, 0, unroll=8)

    @pl.when(i == 0)
    def _():
        issue(0, 0)
        issue(jnp.minimum(1, last_tile), 1)

    for slot in range(2):
        @pl.when(i % 2 == slot)
        def _(slot=slot):
            drain(i, slot)
            y0 = _unpack_rows(ybuf[slot, 0:COMBINE_ROWS, :])
            y1 = _unpack_rows(ybuf[slot, COMBINE_ROWS:rows, :])
            eg = eg_ref[...]
            moe = y0 * eg[:, 0:1] + y1 * eg[:, 1:2]
            x = x_ref[...] + gate2_ref[...] * moe
            if last:
                y = x * lax.rsqrt(jnp.mean(x * x, axis=-1, keepdims=True) + EPS) * g_ref[...]

                @pl.when(i < n_ctx_tiles)
                def _():
                    yc_ref[...] = y

                @pl.when(i >= n_ctx_tiles)
                def _():
                    yl_ref[...] = y
            else:
                xo_ref[...] = x
                h_ref[...] = _norm_mod(x, g_ref[...], scale_ref[...], shift_ref[...]).astype(BF16)
            issue(jnp.minimum(i + 2, last_tile), slot)

    @pl.when(i == last_tile)
    def _():
        drain(last_tile, 0)
        drain(last_tile, 1)


def _combine(tok, yb, dest_tiles, x, eg, mod5, g, layer, last):
    tm = COMBINE_ROWS
    row = lambda width: pl.BlockSpec((tm, width), lambda i, d: (i, 0))
    in_specs = [pl.BlockSpec(memory_space=pl.ANY), row(D_MODEL), row(ROUTE_LANES), tok.mod_spec(layer, 5, tm)]
    args = [dest_tiles, yb, x, eg, mod5, g]
    n_ctx_tiles = tok.t_ctx // tm
    if last:
        in_specs += [pl.BlockSpec((1, D_MODEL), lambda i, d: (0, 0))]
        out_specs = [pl.BlockSpec((tm, D_MODEL), lambda i, d: (jnp.minimum(i, n_ctx_tiles - 1), 0)),
                     pl.BlockSpec((tm, D_MODEL), lambda i, d: (jnp.maximum(i - n_ctx_tiles, 0), 0))]
        out_shape = [jax.ShapeDtypeStruct((tok.t_ctx, D_MODEL), F32), jax.ShapeDtypeStruct((tok.t_lat, D_MODEL), F32)]
    else:
        in_specs += [pl.BlockSpec((None, 1, D_MODEL), lambda i, d: (layer + 1, 0, 0)),
                     tok.mod_spec(layer + 1, 1, tm), tok.mod_spec(layer + 1, 0, tm)]
        args += [mod5, mod5]
        out_specs = [row(D_MODEL), row(D_MODEL)]
        out_shape = [jax.ShapeDtypeStruct((tok.t, D_MODEL), F32), jax.ShapeDtypeStruct((tok.t, D_MODEL), BF16)]
    return pl.pallas_call(
        functools.partial(_combine_kernel, last=last, n_ctx_tiles=n_ctx_tiles),
        grid_spec=pltpu.PrefetchScalarGridSpec(
            num_scalar_prefetch=1,
            grid=(tok.t // tm,),
            in_specs=in_specs,
            out_specs=out_specs,
            scratch_shapes=[pltpu.VMEM((2, tm * TOP_K, PACKED), U32), pltpu.SemaphoreType.DMA((2,))],
        ),
        out_shape=out_shape,
        compiler_params=_params(("arbitrary",)),
        name="combine_final" if last else "combine_norm",
    )(*args)


def _rope_tables(n_tokens):
    pairs = DA_QK_DIM // 4
    rows = n_tokens // GRID_W
    row_ids = jnp.repeat(jnp.arange(rows, dtype=F32), GRID_W)
    col_ids = jnp.tile(jnp.arange(GRID_W, dtype=F32), rows)
    inv = 1.0 / (ROPE_BASE ** (jnp.arange(pairs, dtype=F32) / pairs))
    ang = jnp.concatenate([row_ids[:, None] * inv, col_ids[:, None] * inv], axis=-1)
    cos, sin = jnp.cos(ang), jnp.sin(ang)
    reps = LANE // DA_QK_DIM
    cos_full = jnp.tile(jnp.concatenate([cos, cos], axis=-1), (1, reps))
    sin_signed = jnp.tile(jnp.concatenate([-sin, sin], axis=-1), (1, reps))
    return cos_full, sin_signed


def _block_diag(w):
    eye = jnp.eye(RG_HEADS, dtype=w.dtype)
    return jnp.einsum("ldhij,hg->ldhigj", w, eye).reshape(DEPTH, 2, RG_WIDTH, RG_WIDTH)


def kernel(x_prompt, x_sample, c, c_ctx, cache_k, cache_v, state_rglru, w_mod, b_mod, norm1, norm2, w_in, w_out,
           conv_dw_w, conv_dw_b, conv_ln_g, conv_ln_b, da_lambda, da_norm, rg_conv_w, rg_conv_b, rg_w_a, rg_b_a,
           rg_w_x, rg_b_x, rg_lambda, moe_w_grp, moe_b_grp, moe_w_exp, moe_b_exp, moe_w_gate, moe_w_up,
           moe_w_down, final_norm):
    n_ctx_seq, s_ctx, d = x_prompt.shape
    n_lat_seq, s_lat, _ = x_sample.shape
    past = cache_k.shape[2]
    assert d == D_MODEL and w_in.shape == (DEPTH, D_MODEL, IN_WIDTH) and c.shape[0] + 1 <= N_COND
    assert s_lat % s_ctx == 0 and (n_ctx_seq * s_ctx) % s_lat == 0 and s_ctx % 256 == 0 and s_lat % 512 == 0
    tok = _Tokens(n_ctx_seq, s_ctx, n_lat_seq, s_lat)

    cond = jnp.concatenate([c_ctx[None, :], c, jnp.zeros((N_COND - 1 - n_lat_seq, d), F32)], axis=0)
    mod = _modulation(cond, w_mod, b_mod)
    mod5 = mod.reshape(DEPTH, N_COND, N_MOD, 1, D_MODEL)

    w_out_b = w_out.astype(BF16)
    rg = {
        "conv_w": rg_conv_w, "conv_b": rg_conv_b,
        "w_a": _block_diag(rg_w_a).astype(BF16), "b_a": rg_b_a,
        "w_x": _block_diag(rg_w_x).astype(BF16), "b_x": rg_b_x,
        "lam": rg_lambda,
    }
    w_route = jnp.concatenate(
        [moe_w_grp, moe_w_exp, jnp.zeros((DEPTH, d, ROUTE_LANES - N_GROUPS - N_EXPERTS), F32)], axis=-1)
    b_route = jnp.concatenate(
        [moe_b_grp, moe_b_exp, jnp.zeros((DEPTH, ROUTE_LANES - N_GROUPS - N_EXPERTS), F32)], axis=-1)
    b_route = b_route.reshape(DEPTH, 1, ROUTE_LANES)
    w_route_b = w_route.astype(BF16)
    norm1_r = norm1.reshape(DEPTH, 1, d)
    norm2_r = norm2.reshape(DEPTH, 1, d)
    dw_b = conv_dw_b.reshape(DEPTH, 1, CONV_WIDTH)
    ln_g = conv_ln_g.reshape(DEPTH, 1, CONV_WIDTH)
    ln_b = conv_ln_b.reshape(DEPTH, 1, CONV_WIDTH)
    da_norm_r = da_norm.reshape(DEPTH, 1, DA_V_DIM)
    cache_k_r = cache_k.reshape(n_lat_seq, DEPTH, past, DA_WIDTH)
    cache_v_r = cache_v.reshape(n_lat_seq, DEPTH, past, DA_WIDTH)
    cos, sin = _rope_tables(s_lat)
    h0_ctx = jnp.zeros((n_ctx_seq, 2, RG_WIDTH), F32)
    h0_ctx_spec = pl.BlockSpec((None, 2, RG_WIDTH), lambda i: (i, 0, 0))

    x = jnp.concatenate([x_prompt.reshape(tok.t_ctx, d), x_sample.reshape(tok.t_lat, d)], axis=0)
    h = _prenorm(tok, x, norm1_r, mod5, 0)
    new_k, new_v, new_h = [], [], []
    for layer in range(DEPTH):
        lam_init = 0.8 - 0.6 * math.exp(-0.3 * layer)
        p = _inproj(tok, h, w_in, layer)
        conv_ctx = _conformer_conv(tok, p, conv_dw_w, dw_b, ln_g, ln_b, layer, False)
        conv_lat = _conformer_conv(tok, p, conv_dw_w, dw_b, ln_g, ln_b, layer, True)
        da_ctx, k_l, v_l = _attn_ctx(tok, p, da_lambda, da_norm_r, layer, lam_init)
        new_k.append(k_l)
        new_v.append(v_l)
        da_lat = _attn_lat(tok, p, cache_k_r, cache_v_r, cos, sin, da_lambda, da_norm_r, layer, lam_init)
        rg_ctx, last_ctx = _rglru(tok, p, h0_ctx, h0_ctx_spec, rg, layer, False)
        h0_lat_spec = pl.BlockSpec((None, None, 2, RG_WIDTH), lambda i, layer=layer: (i, layer, 0, 0))
        rg_lat, _ = _rglru(tok, p, state_rglru, h0_lat_spec, rg, layer, True)
        new_h.append(last_ctx)

        x, h2, ei, eg = _outproj(tok, (conv_ctx, da_ctx, rg_ctx), (conv_lat, da_lat, rg_lat), x, w_out_b, norm2_r,
                                 mod5, w_route_b, b_route, layer)

        blk_expert, n_used, src_off, tok_sorted, dest = _dispatch_plan(ei[:, :TOP_K])
        yb = _expert_ffn(h2, blk_expert, n_used, src_off, tok_sorted, moe_w_gate, moe_w_up, moe_w_down, layer)
        dest_tiles = dest.reshape(tok.t // COMBINE_ROWS, COMBINE_ROWS, TOP_K).transpose(0, 2, 1).reshape(-1)
        last = layer == DEPTH - 1
        if last:
            y_ctx, y_lat = _combine(tok, yb, dest_tiles, x, eg, mod5, final_norm.reshape(1, d), layer, True)
        else:
            x, h = _combine(tok, yb, dest_tiles, x, eg, mod5, norm1_r, layer, False)

    y_prompt = y_ctx.reshape(n_ctx_seq, s_ctx, d)
    y_sample = y_lat.reshape(n_lat_seq, s_lat, d)
    return (y_prompt, y_sample, jnp.stack(new_k, axis=1), jnp.stack(new_v, axis=1), jnp.stack(new_h, axis=1))
```

```python
import functools
import math

import jax
import jax.numpy as jnp
from jax import lax
from jax.experimental import pallas as pl
from jax.experimental.pallas import tpu as pltpu

F32 = jnp.float32
BF16 = jnp.bfloat16

D_MODEL = 2048
DEPTH = 4
GRID_W = 64
CONV_WIDTH = 512
CONV_K = 31
DA_HEADS = 8
DA_QK_DIM = 64
DA_V_DIM = 128
DA_WIDTH = DA_HEADS * DA_V_DIM
RG_WIDTH = 512
RG_HEADS = 8
RG_HEAD_DIM = RG_WIDTH // RG_HEADS
RG_CONV_K = 4
RG_C = 8.0
IN_WIDTH = 2 * CONV_WIDTH + 3 * DA_WIDTH + 2 * RG_WIDTH
MIX_WIDTH = CONV_WIDTH + DA_WIDTH + RG_WIDTH
N_GROUPS = 4
EXPERTS_PER_GROUP = 8
N_EXPERTS = N_GROUPS * EXPERTS_PER_GROUP
TOP_K = 2
D_EXPERT = 512
ROPE_BASE = 10000.0
EPS = 1e-6
N_COND = 8
N_MOD = 6

LANE = 128
SUBLANE = 8
COL_Q = (2 * CONV_WIDTH) // LANE
COL_K = COL_Q + DA_WIDTH // LANE
COL_V = COL_K + DA_WIDTH // LANE
COL_RX = (2 * CONV_WIDTH + 3 * DA_WIDTH) // RG_WIDTH
COL_RZ = COL_RX + 1

EXPERT_ROWS = 256
ISSUE_PARTS = 4
COMBINE_ROWS = 128
VMEM_LIMIT = 56 * 1024 * 1024


def _params(sem, vmem=VMEM_LIMIT):
    return pltpu.CompilerParams(dimension_semantics=sem, vmem_limit_bytes=vmem)


PACKED = D_MODEL // 2
U32 = jnp.uint32


def _pack_rows(v):
    lo = pltpu.bitcast(v[:, :PACKED].astype(BF16).astype(F32), U32)
    hi = pltpu.bitcast(v[:, PACKED:].astype(BF16).astype(F32), U32)
    return hi | (lo >> 16)


def _unpack_rows(w):
    lo = pltpu.bitcast(w << 16, F32)
    hi = pltpu.bitcast(w & jnp.uint32(0xFFFF0000), F32)
    return jnp.concatenate([lo, hi], axis=1)


def _norm_mod(x, g, scale, shift):
    y = x * lax.rsqrt(jnp.mean(x * x, axis=-1, keepdims=True) + EPS)
    return (y * g) * (1.0 + scale) + shift


class _Tokens:
    def __init__(self, n_ctx_seq, s_ctx, n_lat_seq, s_lat):
        self.n_ctx_seq, self.s_ctx, self.n_lat_seq, self.s_lat = n_ctx_seq, s_ctx, n_lat_seq, s_lat
        self.t_ctx = n_ctx_seq * s_ctx
        self.t_lat = n_lat_seq * s_lat
        self.t = self.t_ctx + self.t_lat

    def cond(self, i, tm):
        n_ctx_tiles = self.t_ctx // tm
        per_seq = self.s_lat // tm
        return jnp.where(i < n_ctx_tiles, 0, 1 + (i - n_ctx_tiles) // per_seq)

    def mod_spec(self, layer, which, tm):
        return pl.BlockSpec((None, None, None, 1, D_MODEL),
                            lambda i, *_: (layer, self.cond(i, tm), which, 0, 0))


def _mod_kernel(c_ref, w_ref, b_ref, o_ref):
    c = c_ref[...]
    s = c * jax.nn.sigmoid(c)
    o_ref[...] = jnp.dot(s.astype(BF16), w_ref[...].astype(BF16), preferred_element_type=F32) + b_ref[...]


def _modulation(cond, w_mod, b_mod):
    tn = 1024
    n = N_MOD * D_MODEL
    return pl.pallas_call(
        _mod_kernel,
        grid=(DEPTH, n // tn),
        in_specs=[
            pl.BlockSpec((N_COND, D_MODEL), lambda l, j: (0, 0)),
            pl.BlockSpec((None, D_MODEL, tn), lambda l, j: (l, 0, j)),
            pl.BlockSpec((None, 1, tn), lambda l, j: (l, 0, j)),
        ],
        out_specs=pl.BlockSpec((None, N_COND, tn), lambda l, j: (l, 0, j)),
        out_shape=jax.ShapeDtypeStruct((DEPTH, N_COND, n), F32),
        compiler_params=_params(("parallel", "parallel")),
        name="modulation",
    )(cond, w_mod, b_mod.reshape(DEPTH, 1, n))


def _prenorm_kernel(x_ref, g_ref, scale_ref, shift_ref, h_ref):
    h_ref[...] = _norm_mod(x_ref[...], g_ref[...], scale_ref[...], shift_ref[...]).astype(BF16)


def _prenorm(tok, x, norm1, mod5, layer):
    tm = 256
    return pl.pallas_call(
        _prenorm_kernel,
        grid=(tok.t // tm,),
        in_specs=[
            pl.BlockSpec((tm, D_MODEL), lambda i: (i, 0)),
            pl.BlockSpec((None, 1, D_MODEL), lambda i: (layer, 0, 0)),
            tok.mod_spec(layer, 1, tm),
            tok.mod_spec(layer, 0, tm),
        ],
        out_specs=pl.BlockSpec((tm, D_MODEL), lambda i: (i, 0)),
        out_shape=jax.ShapeDtypeStruct((tok.t, D_MODEL), BF16),
        compiler_params=_params(("parallel",)),
        name="prenorm",
    )(x, norm1, mod5, mod5)


def _inproj_kernel(h_ref, w_ref, p_ref, wb_ref):
    @pl.when(pl.program_id(1) == 0)
    def _():
        wb_ref[...] = w_ref[...].astype(BF16)

    p_ref[...] = jnp.dot(h_ref[...], wb_ref[...], preferred_element_type=F32)


def _inproj(tok, h, w_in, layer):
    tm, tn = (1024 if tok.t % 1024 == 0 else 512), 1024
    return pl.pallas_call(
        _inproj_kernel,
        grid=(IN_WIDTH // tn, tok.t // tm),
        in_specs=[
            pl.BlockSpec((tm, D_MODEL), lambda j, i: (i, 0)),
            pl.BlockSpec((None, D_MODEL, tn), lambda j, i: (layer, 0, j)),
        ],
        out_specs=pl.BlockSpec((tm, tn), lambda j, i: (i, j)),
        out_shape=jax.ShapeDtypeStruct((tok.t, IN_WIDTH), F32),
        scratch_shapes=[pltpu.VMEM((D_MODEL, tn), BF16)],
        compiler_params=_params(("arbitrary", "arbitrary")),
        name="inproj",
    )(h, w_in)


CONV_PAD = 16
CONV_CHUNK = 64
GLU_CHUNK = 128


def _conv_kernel(ca_ref, cg_ref, w_ref, b_ref, g_ref, beta_ref, o_ref, upad_ref, *, seq):
    zeros = jnp.zeros((CONV_PAD, CONV_WIDTH), F32)
    upad_ref[pl.ds(0, CONV_PAD), :] = zeros
    upad_ref[pl.ds(seq + CONV_PAD, CONV_PAD), :] = zeros

    def glu(c, carry):
        r = pl.multiple_of(c * GLU_CHUNK, GLU_CHUNK)
        u = ca_ref[pl.ds(r, GLU_CHUNK), :] * jax.nn.sigmoid(cg_ref[pl.ds(r, GLU_CHUNK), :])
        upad_ref[pl.ds(r + CONV_PAD, GLU_CHUNK), :] = u
        return carry

    lax.fori_loop(0, seq // GLU_CHUNK, glu, 0)

    half = CONV_K // 2

    def conv(c, carry):
        r = pl.multiple_of(c * CONV_CHUNK, CONV_CHUNK)
        parts = []
        for lt in range(CONV_WIDTH // LANE):
            lanes = slice(lt * LANE, (lt + 1) * LANE)
            win = upad_ref[pl.ds(r, CONV_CHUNK + 2 * CONV_PAD), lanes]
            acc = jnp.zeros((CONV_CHUNK, LANE), F32) + b_ref[:, lanes]
            rows = CONV_CHUNK + 2 * CONV_PAD
            for s in range(SUBLANE):
                shifted = win if s == 0 else pltpu.roll(win, rows - s, axis=0)
                for a in range(2 * CONV_PAD // SUBLANE):
                    j = a * SUBLANE + s - (CONV_PAD - half)
                    if 0 <= j < CONV_K:
                        acc = acc + w_ref[j:j + 1, lanes] * shifted[a * SUBLANE:a * SUBLANE + CONV_CHUNK, :]
            parts.append(acc)
        acc = jnp.concatenate(parts, axis=1)
        mu = jnp.mean(acc, axis=-1, keepdims=True)
        d = acc - mu
        var = jnp.mean(d * d, axis=-1, keepdims=True)
        un = d * lax.rsqrt(var + EPS) * g_ref[...] + beta_ref[...]
        o_ref[pl.ds(r, CONV_CHUNK), :] = (un * jax.nn.sigmoid(un)).astype(BF16)
        return carry

    lax.fori_loop(0, seq // CONV_CHUNK, conv, 0)


def _conformer_conv(tok, p, dw_w, dw_b, ln_g, ln_b, layer, latent):
    seq = tok.s_lat if latent else tok.s_ctx
    nseq = tok.n_lat_seq if latent else tok.n_ctx_seq
    row0 = tok.t_ctx // seq if latent else 0
    vec = pl.BlockSpec((None, 1, CONV_WIDTH), lambda i: (layer, 0, 0))
    return pl.pallas_call(
        functools.partial(_conv_kernel, seq=seq),
        grid=(nseq,),
        in_specs=[
            pl.BlockSpec((seq, CONV_WIDTH), lambda i: (row0 + i, 0)),
            pl.BlockSpec((seq, CONV_WIDTH), lambda i: (row0 + i, 1)),
            pl.BlockSpec((None, CONV_K, CONV_WIDTH), lambda i: (layer, 0, 0)),
            vec, vec, vec,
        ],
        out_specs=pl.BlockSpec((seq, CONV_WIDTH), lambda i: (i, 0)),
        out_shape=jax.ShapeDtypeStruct((nseq * seq, CONV_WIDTH), BF16),
        scratch_shapes=[pltpu.VMEM((seq + 2 * CONV_PAD, CONV_WIDTH), F32)],
        compiler_params=_params(("parallel",)),
        name="conformer_conv_lat" if latent else "conformer_conv_ctx",
    )(p, p, dw_w, dw_b, ln_g, ln_b)


ATTN_SCALE = DA_QK_DIM ** -0.5
LOG2_E = math.log2(math.e)
ROPE_CHUNK = 256


def _rope(x, cos, sin_signed):
    lane = lax.broadcasted_iota(jnp.int32, x.shape, 1)
    first = (lane % DA_QK_DIM) < (DA_QK_DIM // 2)
    partner = jnp.where(first, pltpu.roll(x, LANE - DA_QK_DIM // 2, axis=1), pltpu.roll(x, DA_QK_DIM // 2, axis=1))
    return x * cos + partner * sin_signed


def _lambda(lamv_ref, lam_init):
    lv = lamv_ref[...]
    a = jnp.sum(lv[0:1, :] * lv[1:2, :], axis=-1, keepdims=True)
    b = jnp.sum(lv[2:3, :] * lv[3:4, :], axis=-1, keepdims=True)
    return jnp.exp(a) - jnp.exp(b) + lam_init


ATTN_ROWS = 128


def _diff_attn(q, kb, vb, lam, g, lam_init, rows=ATTN_ROWS):
    n = q.shape[0] // rows
    outs = [_diff_attn_rows(q[c * rows:(c + 1) * rows], kb, vb, lam, g, lam_init) for c in range(n)]
    return outs[0] if n == 1 else jnp.concatenate(outs, axis=0)


def _diff_attn_rows(q, kb, vb, lam, g, lam_init):
    tq = q.shape[0]
    lane = lax.broadcasted_iota(jnp.int32, q.shape, 1)
    qs = q * (ATTN_SCALE * LOG2_E)
    q1 = jnp.where(lane < DA_QK_DIM, qs, 0.0).astype(BF16)
    q2 = jnp.where(lane >= DA_QK_DIM, qs, 0.0).astype(BF16)
    qq = jnp.concatenate([q1, q2], axis=0)
    s = lax.dot_general(qq, kb, (((1,), (1,)), ((), ())), preferred_element_type=F32)
    m = jnp.max(s, axis=-1, keepdims=True)
    e = jnp.exp2(s - m)
    inv = 1.0 / jnp.sum(e, axis=-1, keepdims=True)
    ov = jnp.dot(e.astype(BF16), vb, preferred_element_type=F32) * inv
    o = ov[:tq] - lam * ov[tq:]
    y = o * lax.rsqrt(jnp.mean(o * o, axis=-1, keepdims=True) + EPS) * g
    return (y * (1.0 - lam_init)).astype(BF16)


def _attn_ctx_kernel(q_ref, k_ref, v_ref, lamv_ref, g_ref, o_ref, ko_ref, vo_ref, *, lam_init):
    lam = _lambda(lamv_ref, lam_init)
    for h in range(DA_HEADS):
        cols = slice(h * LANE, (h + 1) * LANE)
        k = k_ref[:, cols]
        v = v_ref[:, cols]
        ko_ref[:, h, :] = k
        vo_ref[:, h, :] = v
        o_ref[:, cols] = _diff_attn(q_ref[:, cols], k.astype(BF16), v.astype(BF16), lam, g_ref[...], lam_init,
                                    rows=q_ref.shape[0])


def _attn_ctx(tok, p, da_lambda, da_norm, layer, lam_init):
    s = tok.s_ctx
    cache_shape = jax.ShapeDtypeStruct((tok.n_ctx_seq, s, DA_HEADS, DA_V_DIM), F32)
    cache_spec = pl.BlockSpec((None, s, DA_HEADS, DA_V_DIM), lambda b: (b, 0, 0, 0))
    return pl.pallas_call(
        functools.partial(_attn_ctx_kernel, lam_init=lam_init),
        grid=(tok.n_ctx_seq,),
        in_specs=[
            pl.BlockSpec((s, DA_WIDTH), lambda b: (b, COL_Q * LANE // DA_WIDTH)),
            pl.BlockSpec((s, DA_WIDTH), lambda b: (b, COL_K * LANE // DA_WIDTH)),
            pl.BlockSpec((s, DA_WIDTH), lambda b: (b, COL_V * LANE // DA_WIDTH)),
            pl.BlockSpec((None, 4, DA_QK_DIM), lambda b: (layer, 0, 0)),
            pl.BlockSpec((None, 1, DA_V_DIM), lambda b: (layer, 0, 0)),
        ],
        out_specs=[pl.BlockSpec((s, DA_WIDTH), lambda b: (b, 0)), cache_spec, cache_spec],
        out_shape=[jax.ShapeDtypeStruct((tok.t_ctx, DA_WIDTH), BF16), cache_shape, cache_shape],
        compiler_params=_params(("parallel",)),
        name="diff_attn_ctx",
    )(p, p, p, da_lambda, da_norm)


def _attn_lat_kernel(q_ref, k_ref, v_ref, ck_ref, cv_ref, cos_ref, sin_ref, lamv_ref, g_ref, o_ref,
                     kall_ref, vall_ref, *, lam_init, tq, s_lat):
    qi = pl.program_id(2)

    @pl.when(qi == 0)
    def _():
        def fill(c, carry):
            r = pl.multiple_of(c * ROPE_CHUNK, ROPE_CHUNK)
            rows = pl.ds(r, ROPE_CHUNK)
            kall_ref[rows, :] = _rope(k_ref[rows, :], cos_ref[rows, :], sin_ref[rows, :]).astype(BF16)
            vall_ref[rows, :] = v_ref[rows, :].astype(BF16)
            return carry

        lax.fori_loop(0, s_lat // ROPE_CHUNK, fill, 0)
        past = ck_ref.shape[0]
        kall_ref[pl.ds(s_lat, past), :] = ck_ref[...].astype(BF16)
        vall_ref[pl.ds(s_lat, past), :] = cv_ref[...].astype(BF16)

    rows = pl.ds(pl.multiple_of(qi * tq, tq), tq)
    q = _rope(q_ref[...], cos_ref[rows, :], sin_ref[rows, :])
    lam = _lambda(lamv_ref, lam_init)
    o_ref[...] = _diff_attn(q, kall_ref[...], vall_ref[...], lam, g_ref[...], lam_init)


def _attn_lat(tok, p, cache_k, cache_v, cos, sin, da_lambda, da_norm, layer, lam_init):
    s = tok.s_lat
    tq = min(1024, s)
    past = cache_k.shape[2]
    q0 = tok.t_ctx // tq
    kv0 = tok.t_ctx // s
    nq = s // tq
    return pl.pallas_call(
        functools.partial(_attn_lat_kernel, lam_init=lam_init, tq=tq, s_lat=s),
        grid=(tok.n_lat_seq, DA_HEADS, nq),
        in_specs=[
            pl.BlockSpec((tq, LANE), lambda b, h, i: (q0 + b * nq + i, COL_Q + h)),
            pl.BlockSpec((s, LANE), lambda b, h, i: (kv0 + b, COL_K + h)),
            pl.BlockSpec((s, LANE), lambda b, h, i: (kv0 + b, COL_V + h)),
            pl.BlockSpec((None, None, past, LANE), lambda b, h, i: (b, layer, 0, h)),
            pl.BlockSpec((None, None, past, LANE), lambda b, h, i: (b, layer, 0, h)),
            pl.BlockSpec((s, LANE), lambda b, h, i: (0, 0)),
            pl.BlockSpec((s, LANE), lambda b, h, i: (0, 0)),
            pl.BlockSpec((None, 4, DA_QK_DIM), lambda b, h, i: (layer, 0, 0)),
            pl.BlockSpec((None, 1, DA_V_DIM), lambda b, h, i: (layer, 0, 0)),
        ],
        out_specs=pl.BlockSpec((tq, LANE), lambda b, h, i: (b * nq + i, h)),
        out_shape=jax.ShapeDtypeStruct((tok.t_lat, DA_WIDTH), BF16),
        scratch_shapes=[pltpu.VMEM((s + past, LANE), BF16), pltpu.VMEM((s + past, LANE), BF16)],
        compiler_params=_params(("parallel", "parallel", "arbitrary")),
        name="diff_attn_lat",
    )(p, p, p, cache_k, cache_v, cos, sin, da_lambda, da_norm)


RG_PAD = SUBLANE
RG_CHUNK = 128


def _scan_tile(a, u, hprev, reverse):
    row = lax.broadcasted_iota(jnp.int32, a.shape, 0)
    for k in (1, 2, 4):
        if reverse:
            keep = row < SUBLANE - k
            shift = SUBLANE - k
        else:
            keep = row >= k
            shift = k
        a_prev = jnp.where(keep, pltpu.roll(a, shift, axis=0), 1.0)
        u_prev = jnp.where(keep, pltpu.roll(u, shift, axis=0), 0.0)
        u = u + a * u_prev
        a = a * a_prev
    return a * hprev + u


def _rglru_kernel(rx_ref, rz_ref, cw_ref, cb_ref, wa_ref, ba_ref, wx_ref, bx_ref, lam_ref, h0_ref,
                  o_ref, last_ref, xpad_ref, a_ref, u_ref, hs_ref, *, seq):
    zeros = jnp.zeros((RG_PAD, RG_WIDTH), F32)
    xpad_ref[pl.ds(0, RG_PAD), :] = zeros
    xpad_ref[pl.ds(seq + RG_PAD, RG_PAD), :] = zeros

    def copy_in(c, carry):
        r = pl.multiple_of(c * RG_CHUNK, RG_CHUNK)
        xpad_ref[pl.ds(r + RG_PAD, RG_CHUNK), :] = rx_ref[pl.ds(r, RG_CHUNK), :]
        return carry

    lax.fori_loop(0, seq // RG_CHUNK, copy_in, 0)

    n_tiles = seq // SUBLANE
    for d in range(2):
        reverse = d == 1
        lam = lam_ref[d:d + 1, :]
        decay = -RG_C * (jnp.maximum(-lam, 0.0) + jnp.log(1.0 + jnp.exp(-jnp.abs(lam))))
        first_tap = RG_PAD if reverse else RG_PAD - (RG_CONV_K - 1)

        def gates(c, carry, d=d, decay=decay, first_tap=first_tap):
            r = pl.multiple_of(c * RG_CHUNK, RG_CHUNK)
            xc = jnp.zeros((RG_CHUNK, RG_WIDTH), F32) + cb_ref[d:d + 1, :]
            win = xpad_ref[pl.ds(r, RG_CHUNK + 2 * RG_PAD), :]
            for j in range(RG_CONV_K):
                xc = xc + cw_ref[d, j:j + 1, :] * win[first_tap + j:first_tap + j + RG_CHUNK, :]
            xb = xc.astype(BF16)
            rg = jax.nn.sigmoid(jnp.dot(xb, wa_ref[d], preferred_element_type=F32) + ba_ref[d:d + 1, :])
            ig = jax.nn.sigmoid(jnp.dot(xb, wx_ref[d], preferred_element_type=F32) + bx_ref[d:d + 1, :])
            log_a = rg * decay
            a = jnp.exp(log_a)
            a_ref[pl.ds(r, RG_CHUNK), :] = a
            u_ref[pl.ds(r, RG_CHUNK), :] = jnp.sqrt(1.0 - a * a) * ig * xc
            return carry

        lax.fori_loop(0, seq // RG_CHUNK, gates, 0)

        def scan(t, hprev, d=d, reverse=reverse):
            tile = (n_tiles - 1 - t) if reverse else t
            rows = pl.ds(pl.multiple_of(tile * SUBLANE, SUBLANE), SUBLANE)
            h = _scan_tile(a_ref[rows, :], u_ref[rows, :], hprev, reverse)
            if d == 0:
                hs_ref[rows, :] = h
            else:
                hs_ref[rows, :] = hs_ref[rows, :] + h
            return h[0:1, :] if reverse else h[SUBLANE - 1:SUBLANE, :]

        last = lax.fori_loop(0, n_tiles, scan, h0_ref[d:d + 1, :], unroll=4)
        last_ref[d:d + 1, :] = last

    def gate_out(c, carry):
        rows = pl.ds(pl.multiple_of(c * RG_CHUNK, RG_CHUNK), RG_CHUNK)
        o_ref[rows, :] = (hs_ref[rows, :] * jax.nn.gelu(rz_ref[rows, :])).astype(BF16)
        return carry

    lax.fori_loop(0, seq // RG_CHUNK, gate_out, 0)


def _rglru(tok, p, h0, h0_spec, rg, layer, latent):
    seq = tok.s_lat if latent else tok.s_ctx
    nseq = tok.n_lat_seq if latent else tok.n_ctx_seq
    row0 = tok.t_ctx // seq if latent else 0
    vec2 = pl.BlockSpec((None, 2, RG_WIDTH), lambda i: (layer, 0, 0))
    mat2 = pl.BlockSpec((None, 2, RG_WIDTH, RG_WIDTH), lambda i: (layer, 0, 0, 0))
    return pl.pallas_call(
        functools.partial(_rglru_kernel, seq=seq),
        grid=(nseq,),
        in_specs=[
            pl.BlockSpec((seq, RG_WIDTH), lambda i: (row0 + i, COL_RX)),
            pl.BlockSpec((seq, RG_WIDTH), lambda i: (row0 + i, COL_RZ)),
            pl.BlockSpec((None, 2, RG_CONV_K, RG_WIDTH), lambda i: (layer, 0, 0, 0)),
            vec2, mat2, vec2, mat2, vec2, vec2,
            h0_spec,
        ],
        out_specs=[
            pl.BlockSpec((seq, RG_WIDTH), lambda i: (i, 0)),
            pl.BlockSpec((None, 2, RG_WIDTH), lambda i: (i, 0, 0)),
        ],
        out_shape=[
            jax.ShapeDtypeStruct((nseq * seq, RG_WIDTH), BF16),
            jax.ShapeDtypeStruct((nseq, 2, RG_WIDTH), F32),
        ],
        scratch_shapes=[
            pltpu.VMEM((seq + 2 * RG_PAD, RG_WIDTH), F32),
            pltpu.VMEM((seq, RG_WIDTH), F32),
            pltpu.VMEM((seq, RG_WIDTH), F32),
            pltpu.VMEM((seq, RG_WIDTH), F32),
        ],
        compiler_params=_params(("parallel",)),
        name="rglru_lat" if latent else "rglru_ctx",
    )(p, p, rg["conv_w"], rg["conv_b"], rg["w_a"], rg["b_a"], rg["w_x"], rg["b_x"], rg["lam"], h0)


ROUTE_LANES = LANE


def _route(logits):
    lane = lax.broadcasted_iota(jnp.int32, logits.shape, 1).astype(F32)
    neg = -jnp.inf
    big = float(ROUTE_LANES)
    is_grp = lane < N_GROUPS
    gl = jnp.where(is_grp, logits, neg)
    gmax = jnp.max(gl, axis=-1, keepdims=True)
    grp = jnp.min(jnp.where(gl == gmax, lane, big), axis=-1, keepdims=True)
    gsum = jnp.sum(jnp.where(is_grp, jnp.exp(logits - gmax), 0.0), axis=-1, keepdims=True)
    p_grp = 1.0 / gsum
    lo = N_GROUPS + grp * EXPERTS_PER_GROUP
    in_grp = (lane >= lo) & (lane < lo + EXPERTS_PER_GROUP)
    el = jnp.where(in_grp, logits, neg)
    m1 = jnp.max(el, axis=-1, keepdims=True)
    i1 = jnp.min(jnp.where(el == m1, lane, big), axis=-1, keepdims=True)
    el2 = jnp.where(lane == i1, neg, el)
    m2 = jnp.max(el2, axis=-1, keepdims=True)
    i2 = jnp.min(jnp.where(el2 == m2, lane, big), axis=-1, keepdims=True)
    z = jnp.sum(jnp.where(in_grp, jnp.exp(logits - m1), 0.0), axis=-1, keepdims=True)
    w1 = 1.0 / z
    w2 = jnp.exp(m2 - m1) / z
    wsum = w1 + w2
    g1 = w1 / wsum * p_grp
    g2 = w2 / wsum * p_grp
    experts = jnp.where(lane == 0.0, i1 - N_GROUPS, jnp.where(lane == 1.0, i2 - N_GROUPS, 0.0))
    gates = jnp.where(lane == 0.0, g1, jnp.where(lane == 1.0, g2, 0.0))
    return experts.astype(jnp.int32), gates


def _outproj_kernel(conv_c, da_c, rg_c, conv_l, da_l, rg_l, x_ref, w_ref, gate1_ref, g2_ref, scale2_ref,
                    shift2_ref, wr_ref, br_ref, xo_ref, h2_ref, ei_ref, eg_ref, *, n_ctx_tiles):
    is_ctx = pl.program_id(0) < n_ctx_tiles
    conv = jnp.where(is_ctx, conv_c[...], conv_l[...])
    da = jnp.where(is_ctx, da_c[...], da_l[...])
    rg = jnp.where(is_ctx, rg_c[...], rg_l[...])
    mixed = jnp.dot(conv, w_ref[0:CONV_WIDTH, :], preferred_element_type=F32)
    mixed += jnp.dot(da, w_ref[CONV_WIDTH:CONV_WIDTH + DA_WIDTH, :], preferred_element_type=F32)
    mixed += jnp.dot(rg, w_ref[CONV_WIDTH + DA_WIDTH:MIX_WIDTH, :], preferred_element_type=F32)
    x = x_ref[...] + gate1_ref[...] * mixed
    xo_ref[...] = x
    h2 = _norm_mod(x, g2_ref[...], scale2_ref[...], shift2_ref[...])
    h2_ref[...] = _pack_rows(h2)
    logits = jnp.dot(h2.astype(BF16), wr_ref[...], preferred_element_type=F32) + br_ref[...]
    experts, gates = _route(logits)
    ei_ref[...] = experts
    eg_ref[...] = gates


def _outproj(tok, mix_ctx, mix_lat, x, w_out_b, norm2, mod5, w_route_b, b_route, layer):
    tm = 512
    n_ctx_tiles = tok.t_ctx // tm
    row = lambda width: pl.BlockSpec((tm, width), lambda i: (i, 0))
    ctx_row = lambda width: pl.BlockSpec((tm, width), lambda i: (jnp.minimum(i, n_ctx_tiles - 1), 0))
    lat_row = lambda width: pl.BlockSpec((tm, width), lambda i: (jnp.maximum(i - n_ctx_tiles, 0), 0))
    widths = (CONV_WIDTH, DA_WIDTH, RG_WIDTH)
    return pl.pallas_call(
        functools.partial(_outproj_kernel, n_ctx_tiles=n_ctx_tiles),
        grid=(tok.t // tm,),
        in_specs=[ctx_row(w) for w in widths] + [lat_row(w) for w in widths] + [
            row(D_MODEL),
            pl.BlockSpec((None, MIX_WIDTH, D_MODEL), lambda i: (layer, 0, 0), pipeline_mode=pl.Buffered(1)),
            tok.mod_spec(layer, 2, tm),
            pl.BlockSpec((None, 1, D_MODEL), lambda i: (layer, 0, 0)),
            tok.mod_spec(layer, 4, tm),
            tok.mod_spec(layer, 3, tm),
            pl.BlockSpec((None, D_MODEL, ROUTE_LANES), lambda i: (layer, 0, 0)),
            pl.BlockSpec((None, 1, ROUTE_LANES), lambda i: (layer, 0, 0)),
        ],
        out_specs=[row(D_MODEL), row(PACKED), row(ROUTE_LANES), row(ROUTE_LANES)],
        out_shape=[
            jax.ShapeDtypeStruct((tok.t, D_MODEL), F32),
            jax.ShapeDtypeStruct((tok.t, PACKED), U32),
            jax.ShapeDtypeStruct((tok.t, ROUTE_LANES), jnp.int32),
            jax.ShapeDtypeStruct((tok.t, ROUTE_LANES), F32),
        ],
        compiler_params=_params(("parallel",)),
        name="outproj_route",
    )(*mix_ctx, *mix_lat, x, w_out_b, mod5, norm2, mod5, mod5, w_route_b, b_route)


def _dispatch_plan(experts):
    t = experts.shape[0]
    a = t * TOP_K
    n_blocks = a // EXPERT_ROWS + N_EXPERTS
    ids = jnp.arange(N_EXPERTS, dtype=jnp.int32)
    e_flat = experts.reshape(a)
    order = jnp.argsort(e_flat, stable=True).astype(jnp.int32)
    pos = jnp.argsort(order).astype(jnp.int32)
    onehot = e_flat[:, None] == ids[None, :]
    counts = jnp.sum(onehot.astype(jnp.int32), axis=0)
    padded = (counts + EXPERT_ROWS - 1) // EXPERT_ROWS * EXPERT_ROWS
    start = jnp.cumsum(counts) - counts
    ends_p = jnp.cumsum(padded)
    shift = ends_p - padded - start
    dest = pos + jnp.sum(jnp.where(onehot, shift[None, :], 0), axis=1)
    blk_start = jnp.arange(n_blocks, dtype=jnp.int32) * EXPERT_ROWS
    blk_expert = jnp.minimum(jnp.sum((blk_start[:, None] >= ends_p[None, :]).astype(jnp.int32), axis=1),
                             N_EXPERTS - 1)
    src_off = blk_start - jnp.sum(jnp.where(blk_expert[:, None] == ids[None, :], shift[None, :], 0), axis=1)
    tok_sorted = jnp.concatenate([order // TOP_K, jnp.zeros((EXPERT_ROWS,), jnp.int32)])
    n_used = (ends_p[-1] // EXPERT_ROWS).reshape(1)
    return (blk_expert.astype(jnp.int32), n_used.astype(jnp.int32), src_off.astype(jnp.int32),
            tok_sorted.astype(jnp.int32), dest.astype(jnp.int32))


def _expert_kernel(bexp_ref, nused_ref, srcoff_ref, toks_ref, h2_hbm, wg_ref, wu_ref, wd_ref, y_ref,
                   xbuf, wgb_ref, wub_ref, wdb_ref, sem):
    b = pl.program_id(0)
    n_used = nused_ref[0]
    last_blk = n_used - 1

    def row_copy(base, r, slot):
        t = toks_ref[base + r]
        return pltpu.make_async_copy(h2_hbm.at[pl.ds(t, 1), :], xbuf.at[slot, pl.ds(r, 1), :], sem.at[slot])

    def issue(blk, slot, part=None):
        base = srcoff_ref[blk]
        group = EXPERT_ROWS // ISSUE_PARTS
        rows = range(EXPERT_ROWS) if part is None else range(part * group, (part + 1) * group)
        for r in rows:
            row_copy(base, r, slot).start(priority=r % 2)

    def drain(blk, slot):
        base = srcoff_ref[blk]

        def body(r, carry):
            row_copy(base, r, slot).wait()
            return carry
        lax.fori_loop(0, EXPERT_ROWS, body, 0, unroll=8)

    @pl.when(b == 0)
    def _():
        issue(0, 0)
        issue(jnp.minimum(1, last_blk), 1)

    new_expert = (b == 0) | (bexp_ref[b] != bexp_ref[jnp.maximum(b - 1, 0)])

    @pl.when((b < n_used) & new_expert)
    def _():
        wgb_ref[...] = wg_ref[...].astype(BF16)
        wub_ref[...] = wu_ref[...].astype(BF16)
        wdb_ref[...] = wd_ref[...].astype(BF16)

    for slot in range(2):
        @pl.when((b < n_used) & (b % 2 == slot))
        def _(slot=slot):
            drain(b, slot)
            nxt = jnp.minimum(b + 2, last_blk)
            x = _unpack_rows(xbuf[slot]).astype(BF16)
            issue(nxt, slot, 0)
            gate = jnp.dot(x, wgb_ref[...], preferred_element_type=F32)
            issue(nxt, slot, 1)
            up = jnp.dot(x, wub_ref[...], preferred_element_type=F32)
            issue(nxt, slot, 2)
            hid = (gate * jax.nn.sigmoid(gate)) * up
            y_ref[...] = _pack_rows(jnp.dot(hid.astype(BF16), wdb_ref[...], preferred_element_type=F32))
            issue(nxt, slot, 3)

    @pl.when(b == last_blk)
    def _():
        drain(last_blk, 0)
        drain(last_blk, 1)

    @pl.when(b >= n_used)
    def _():
        y_ref[...] = jnp.zeros_like(y_ref)


def _expert_ffn(h2, blk_expert, n_used, src_off, tok_sorted, w_gate, w_up, w_down, layer):
    n_blocks = blk_expert.shape[0]
    wspec_in = pl.BlockSpec((None, None, D_MODEL, D_EXPERT), lambda b, bexp, *_: (layer, bexp[b], 0, 0))
    wspec_out = pl.BlockSpec((None, None, D_EXPERT, D_MODEL), lambda b, bexp, *_: (layer, bexp[b], 0, 0))
    return pl.pallas_call(
        _expert_kernel,
        grid_spec=pltpu.PrefetchScalarGridSpec(
            num_scalar_prefetch=4,
            grid=(n_blocks,),
            in_specs=[pl.BlockSpec(memory_space=pl.ANY), wspec_in, wspec_in, wspec_out],
            out_specs=pl.BlockSpec((EXPERT_ROWS, PACKED), lambda b, *_: (b, 0)),
            scratch_shapes=[
                pltpu.VMEM((2, EXPERT_ROWS, PACKED), U32),
                pltpu.VMEM((D_MODEL, D_EXPERT), BF16),
                pltpu.VMEM((D_MODEL, D_EXPERT), BF16),
                pltpu.VMEM((D_EXPERT, D_MODEL), BF16),
                pltpu.SemaphoreType.DMA((2,)),
            ],
        ),
        out_shape=jax.ShapeDtypeStruct((n_blocks * EXPERT_ROWS, PACKED), U32),
        compiler_params=_params(("arbitrary",)),
        name="expert_ffn",
    )(blk_expert, n_used, src_off, tok_sorted, h2, w_gate, w_up, w_down)


def _combine_kernel(dest_ref, yb_hbm, x_ref, eg_ref, gate2_ref, g_ref, *rest, last, n_ctx_tiles):
    if last:
        yc_ref, yl_ref, ybuf, sem = rest
    else:
        scale_ref, shift_ref, xo_ref, h_ref, ybuf, sem = rest
    i = pl.program_id(0)
    last_tile = pl.num_programs(0) - 1
    rows = COMBINE_ROWS * TOP_K

    def row_copy(base, r, slot):
        d = dest_ref[base + r]
        return pltpu.make_async_copy(yb_hbm.at[pl.ds(d, 1), :], ybuf.at[slot, pl.ds(r, 1), :], sem.at[slot])

    def issue(tile, slot):
        base = tile * rows
        for r in range(rows):
            row_copy(base, r, slot).start(priority=r % 2)

    def drain(tile, slot):
        base = tile * rows

        def body(r, carry):
            row_copy(base, r, slot).wait()
            return carry
        lax.fori_loop(0, rows, body, 0, unroll=8)

    @pl.when(i == 0)
    def _():
        issue(0, 0)
        issue(jnp.minimum(1, last_tile), 1)

    for slot in range(2):
        @pl.when(i % 2 == slot)
        def _(slot=slot):
            drain(i, slot)
            y0 = _unpack_rows(ybuf[slot, 0:COMBINE_ROWS, :])
            y1 = _unpack_rows(ybuf[slot, COMBINE_ROWS:rows, :])
            eg = eg_ref[...]
            moe = y0 * eg[:, 0:1] + y1 * eg[:, 1:2]
            x = x_ref[...] + gate2_ref[...] * moe
            if last:
                y = x * lax.rsqrt(jnp.mean(x * x, axis=-1, keepdims=True) + EPS) * g_ref[...]

                @pl.when(i < n_ctx_tiles)
                def _():
                    yc_ref[...] = y

                @pl.when(i >= n_ctx_tiles)
                def _():
                    yl_ref[...] = y
            else:
                xo_ref[...] = x
                h_ref[...] = _norm_mod(x, g_ref[...], scale_ref[...], shift_ref[...]).astype(BF16)
            issue(jnp.minimum(i + 2, last_tile), slot)

    @pl.when(i == last_tile)
    def _():
        drain(last_tile, 0)
        drain(last_tile, 1)


def _combine(tok, yb, dest_tiles, x, eg, mod5, g, layer, last):
    tm = COMBINE_ROWS
    row = lambda width: pl.BlockSpec((tm, width), lambda i, d: (i, 0))
    in_specs = [pl.BlockSpec(memory_space=pl.ANY), row(D_MODEL), row(ROUTE_LANES), tok.mod_spec(layer, 5, tm)]
    args = [dest_tiles, yb, x, eg, mod5, g]
    n_ctx_tiles = tok.t_ctx // tm
    if last:
        in_specs += [pl.BlockSpec((1, D_MODEL), lambda i, d: (0, 0))]
        out_specs = [pl.BlockSpec((tm, D_MODEL), lambda i, d: (jnp.minimum(i, n_ctx_tiles - 1), 0)),
                     pl.BlockSpec((tm, D_MODEL), lambda i, d: (jnp.maximum(i - n_ctx_tiles, 0), 0))]
        out_shape = [jax.ShapeDtypeStruct((tok.t_ctx, D_MODEL), F32), jax.ShapeDtypeStruct((tok.t_lat, D_MODEL), F32)]
    else:
        in_specs += [pl.BlockSpec((None, 1, D_MODEL), lambda i, d: (layer + 1, 0, 0)),
                     tok.mod_spec(layer + 1, 1, tm), tok.mod_spec(layer + 1, 0, tm)]
        args += [mod5, mod5]
        out_specs = [row(D_MODEL), row(D_MODEL)]
        out_shape = [jax.ShapeDtypeStruct((tok.t, D_MODEL), F32), jax.ShapeDtypeStruct((tok.t, D_MODEL), BF16)]
    return pl.pallas_call(
        functools.partial(_combine_kernel, last=last, n_ctx_tiles=n_ctx_tiles),
        grid_spec=pltpu.PrefetchScalarGridSpec(
            num_scalar_prefetch=1,
            grid=(tok.t // tm,),
            in_specs=in_specs,
            out_specs=out_specs,
            scratch_shapes=[pltpu.VMEM((2, tm * TOP_K, PACKED), U32), pltpu.SemaphoreType.DMA((2,))],
        ),
        out_shape=out_shape,
        compiler_params=_params(("arbitrary",)),
        name="combine_final" if last else "combine_norm",
    )(*args)


def _rope_tables(n_tokens):
    pairs = DA_QK_DIM // 4
    rows = n_tokens // GRID_W
    row_ids = jnp.repeat(jnp.arange(rows, dtype=F32), GRID_W)
    col_ids = jnp.tile(jnp.arange(GRID_W, dtype=F32), rows)
    inv = 1.0 / (ROPE_BASE ** (jnp.arange(pairs, dtype=F32) / pairs))
    ang = jnp.concatenate([row_ids[:, None] * inv, col_ids[:, None] * inv], axis=-1)
    cos, sin = jnp.cos(ang), jnp.sin(ang)
    reps = LANE // DA_QK_DIM
    cos_full = jnp.tile(jnp.concatenate([cos, cos], axis=-1), (1, reps))
    sin_signed = jnp.tile(jnp.concatenate([-sin, sin], axis=-1), (1, reps))
    return cos_full, sin_signed


def _block_diag(w):
    eye = jnp.eye(RG_HEADS, dtype=w.dtype)
    return jnp.einsum("ldhij,hg->ldhigj", w, eye).reshape(DEPTH, 2, RG_WIDTH, RG_WIDTH)


def kernel(x_prompt, x_sample, c, c_ctx, cache_k, cache_v, state_rglru, w_mod, b_mod, norm1, norm2, w_in, w_out,
           conv_dw_w, conv_dw_b, conv_ln_g, conv_ln_b, da_lambda, da_norm, rg_conv_w, rg_conv_b, rg_w_a, rg_b_a,
           rg_w_x, rg_b_x, rg_lambda, moe_w_grp, moe_b_grp, moe_w_exp, moe_b_exp, moe_w_gate, moe_w_up,
           moe_w_down, final_norm):
    n_ctx_seq, s_ctx, d = x_prompt.shape
    n_lat_seq, s_lat, _ = x_sample.shape
    past = cache_k.shape[2]
    assert d == D_MODEL and w_in.shape == (DEPTH, D_MODEL, IN_WIDTH) and c.shape[0] + 1 <= N_COND
    assert s_lat % s_ctx == 0 and (n_ctx_seq * s_ctx) % s_lat == 0 and s_ctx % 256 == 0 and s_lat % 512 == 0
    tok = _Tokens(n_ctx_seq, s_ctx, n_lat_seq, s_lat)

    cond = jnp.concatenate([c_ctx[None, :], c, jnp.zeros((N_COND - 1 - n_lat_seq, d), F32)], axis=0)
    mod = _modulation(cond, w_mod, b_mod)
    mod5 = mod.reshape(DEPTH, N_COND, N_MOD, 1, D_MODEL)

    w_out_b = w_out.astype(BF16)
    rg = {
        "conv_w": rg_conv_w, "conv_b": rg_conv_b,
        "w_a": _block_diag(rg_w_a).astype(BF16), "b_a": rg_b_a,
        "w_x": _block_diag(rg_w_x).astype(BF16), "b_x": rg_b_x,
        "lam": rg_lambda,
    }
    w_route = jnp.concatenate(
        [moe_w_grp, moe_w_exp, jnp.zeros((DEPTH, d, ROUTE_LANES - N_GROUPS - N_EXPERTS), F32)], axis=-1)
    b_route = jnp.concatenate(
        [moe_b_grp, moe_b_exp, jnp.zeros((DEPTH, ROUTE_LANES - N_GROUPS - N_EXPERTS), F32)], axis=-1)
    b_route = b_route.reshape(DEPTH, 1, ROUTE_LANES)
    w_route_b = w_route.astype(BF16)
    norm1_r = norm1.reshape(DEPTH, 1, d)
    norm2_r = norm2.reshape(DEPTH, 1, d)
    dw_b = conv_dw_b.reshape(DEPTH, 1, CONV_WIDTH)
    ln_g = conv_ln_g.reshape(DEPTH, 1, CONV_WIDTH)
    ln_b = conv_ln_b.reshape(DEPTH, 1, CONV_WIDTH)
    da_norm_r = da_norm.reshape(DEPTH, 1, DA_V_DIM)
    cache_k_r = cache_k.reshape(n_lat_seq, DEPTH, past, DA_WIDTH)
    cache_v_r = cache_v.reshape(n_lat_seq, DEPTH, past, DA_WIDTH)
    cos, sin = _rope_tables(s_lat)
    h0_ctx = jnp.zeros((n_ctx_seq, 2, RG_WIDTH), F32)
    h0_ctx_spec = pl.BlockSpec((None, 2, RG_WIDTH), lambda i: (i, 0, 0))

    x = jnp.concatenate([x_prompt.reshape(tok.t_ctx, d), x_sample.reshape(tok.t_lat, d)], axis=0)
    h = _prenorm(tok, x, norm1_r, mod5, 0)
    new_k, new_v, new_h = [], [], []
    for layer in range(DEPTH):
        lam_init = 0.8 - 0.6 * math.exp(-0.3 * layer)
        p = _inproj(tok, h, w_in, layer)
        conv_ctx = _conformer_conv(tok, p, conv_dw_w, dw_b, ln_g, ln_b, layer, False)
        conv_lat = _conformer_conv(tok, p, conv_dw_w, dw_b, ln_g, ln_b, layer, True)
        da_ctx, k_l, v_l = _attn_ctx(tok, p, da_lambda, da_norm_r, layer, lam_init)
        new_k.append(k_l)
        new_v.append(v_l)
        da_lat = _attn_lat(tok, p, cache_k_r, cache_v_r, cos, sin, da_lambda, da_norm_r, layer, lam_init)
        rg_ctx, last_ctx = _rglru(tok, p, h0_ctx, h0_ctx_spec, rg, layer, False)
        h0_lat_spec = pl.BlockSpec((None, None, 2, RG_WIDTH), lambda i, layer=layer: (i, layer, 0, 0))
        rg_lat, _ = _rglru(tok, p, state_rglru, h0_lat_spec, rg, layer, True)
        new_h.append(last_ctx)

        x, h2, ei, eg = _outproj(tok, (conv_ctx, da_ctx, rg_ctx), (conv_lat, da_lat, rg_lat), x, w_out_b, norm2_r,
                                 mod5, w_route_b, b_route, layer)

        blk_expert, n_used, src_off, tok_sorted, dest = _dispatch_plan(ei[:, :TOP_K])
        yb = _expert_ffn(h2, blk_expert, n_used, src_off, tok_sorted, moe_w_gate, moe_w_up, moe_w_down, layer)
        dest_tiles = dest.reshape(tok.t // COMBINE_ROWS, COMBINE_ROWS, TOP_K).transpose(0, 2, 1).reshape(-1)
        last = layer == DEPTH - 1
        if last:
            y_ctx, y_lat = _combine(tok, yb, dest_tiles, x, eg, mod5, final_norm.reshape(1, d), layer, True)
        else:
            x, h = _combine(tok, yb, dest_tiles, x, eg, mod5, norm1_r, layer, False)

    y_prompt = y_ctx.reshape(n_ctx_seq, s_ctx, d)
    y_sample = y_lat.reshape(n_lat_seq, s_lat, d)
    return (y_prompt, y_sample, jnp.stack(new_k, axis=1), jnp.stack(new_v, axis=1), jnp.stack(new_h, axis=1))
```

```python
import functools
import math

import jax
import jax.numpy as jnp
from jax import lax
from jax.experimental import pallas as pl
from jax.experimental.pallas import tpu as pltpu

F32 = jnp.float32
BF16 = jnp.bfloat16

D_MODEL = 2048
DEPTH = 4
GRID_W = 64
CONV_WIDTH = 512
CONV_K = 31
DA_HEADS = 8
DA_QK_DIM = 64
DA_V_DIM = 128
DA_WIDTH = DA_HEADS * DA_V_DIM
RG_WIDTH = 512
RG_HEADS = 8
RG_HEAD_DIM = RG_WIDTH // RG_HEADS
RG_CONV_K = 4
RG_C = 8.0
IN_WIDTH = 2 * CONV_WIDTH + 3 * DA_WIDTH + 2 * RG_WIDTH
MIX_WIDTH = CONV_WIDTH + DA_WIDTH + RG_WIDTH
N_GROUPS = 4
EXPERTS_PER_GROUP = 8
N_EXPERTS = N_GROUPS * EXPERTS_PER_GROUP
TOP_K = 2
D_EXPERT = 512
ROPE_BASE = 10000.0
EPS = 1e-6
N_COND = 8
N_MOD = 6

LANE = 128
SUBLANE = 8
COL_Q = (2 * CONV_WIDTH) // LANE
COL_K = COL_Q + DA_WIDTH // LANE
COL_V = COL_K + DA_WIDTH // LANE
COL_RX = (2 * CONV_WIDTH + 3 * DA_WIDTH) // RG_WIDTH
COL_RZ = COL_RX + 1

EXPERT_ROWS = 256
ISSUE_PARTS = 4
COMBINE_ROWS = 128
VMEM_LIMIT = 56 * 1024 * 1024


def _params(sem, vmem=VMEM_LIMIT):
    return pltpu.CompilerParams(dimension_semantics=sem, vmem_limit_bytes=vmem)


PACKED = D_MODEL // 2
U32 = jnp.uint32


def _pack_rows(v):
    lo = pltpu.bitcast(v[:, :PACKED].astype(BF16).astype(F32), U32)
    hi = pltpu.bitcast(v[:, PACKED:].astype(BF16).astype(F32), U32)
    return hi | (lo >> 16)


def _unpack_rows(w):
    lo = pltpu.bitcast(w << 16, F32)
    hi = pltpu.bitcast(w & jnp.uint32(0xFFFF0000), F32)
    return jnp.concatenate([lo, hi], axis=1)


def _norm_mod(x, g, scale, shift):
    y = x * lax.rsqrt(jnp.mean(x * x, axis=-1, keepdims=True) + EPS)
    return (y * g) * (1.0 + scale) + shift


class _Tokens:
    def __init__(self, n_ctx_seq, s_ctx, n_lat_seq, s_lat):
        self.n_ctx_seq, self.s_ctx, self.n_lat_seq, self.s_lat = n_ctx_seq, s_ctx, n_lat_seq, s_lat
        self.t_ctx = n_ctx_seq * s_ctx
        self.t_lat = n_lat_seq * s_lat
        self.t = self.t_ctx + self.t_lat

    def cond(self, i, tm):
        n_ctx_tiles = self.t_ctx // tm
        per_seq = self.s_lat // tm
        return jnp.where(i < n_ctx_tiles, 0, 1 + (i - n_ctx_tiles) // per_seq)

    def mod_spec(self, layer, which, tm):
        return pl.BlockSpec((None, None, None, 1, D_MODEL),
                            lambda i, *_: (layer, self.cond(i, tm), which, 0, 0))


def _mod_kernel(c_ref, w_ref, b_ref, o_ref):
    c = c_ref[...]
    s = c * jax.nn.sigmoid(c)
    o_ref[...] = jnp.dot(s.astype(BF16), w_ref[...].astype(BF16), preferred_element_type=F32) + b_ref[...]


def _modulation(cond, w_mod, b_mod):
    tn = 1024
    n = N_MOD * D_MODEL
    return pl.pallas_call(
        _mod_kernel,
        grid=(DEPTH, n // tn),
        in_specs=[
            pl.BlockSpec((N_COND, D_MODEL), lambda l, j: (0, 0)),
            pl.BlockSpec((None, D_MODEL, tn), lambda l, j: (l, 0, j)),
            pl.BlockSpec((None, 1, tn), lambda l, j: (l, 0, j)),
        ],
        out_specs=pl.BlockSpec((None, N_COND, tn), lambda l, j: (l, 0, j)),
        out_shape=jax.ShapeDtypeStruct((DEPTH, N_COND, n), F32),
        compiler_params=_params(("parallel", "parallel")),
        name="modulation",
    )(cond, w_mod, b_mod.reshape(DEPTH, 1, n))


def _prenorm_kernel(x_ref, g_ref, scale_ref, shift_ref, h_ref):
    h_ref[...] = _norm_mod(x_ref[...], g_ref[...], scale_ref[...], shift_ref[...]).astype(BF16)


def _prenorm(tok, x, norm1, mod5, layer):
    tm = 256
    return pl.pallas_call(
        _prenorm_kernel,
        grid=(tok.t // tm,),
        in_specs=[
            pl.BlockSpec((tm, D_MODEL), lambda i: (i, 0)),
            pl.BlockSpec((None, 1, D_MODEL), lambda i: (layer, 0, 0)),
            tok.mod_spec(layer, 1, tm),
            tok.mod_spec(layer, 0, tm),
        ],
        out_specs=pl.BlockSpec((tm, D_MODEL), lambda i: (i, 0)),
        out_shape=jax.ShapeDtypeStruct((tok.t, D_MODEL), BF16),
        compiler_params=_params(("parallel",)),
        name="prenorm",
    )(x, norm1, mod5, mod5)


def _inproj_kernel(h_ref, w_ref, p_ref, wb_ref):
    @pl.when(pl.program_id(1) == 0)
    def _():
        wb_ref[...] = w_ref[...].astype(BF16)

    p_ref[...] = jnp.dot(h_ref[...], wb_ref[...], preferred_element_type=F32)


def _inproj(tok, h, w_in, layer):
    tm, tn = (1024 if tok.t % 1024 == 0 else 512), 1024
    return pl.pallas_call(
        _inproj_kernel,
        grid=(IN_WIDTH // tn, tok.t // tm),
        in_specs=[
            pl.BlockSpec((tm, D_MODEL), lambda j, i: (i, 0)),
            pl.BlockSpec((None, D_MODEL, tn), lambda j, i: (layer, 0, j)),
        ],
        out_specs=pl.BlockSpec((tm, tn), lambda j, i: (i, j)),
        out_shape=jax.ShapeDtypeStruct((tok.t, IN_WIDTH), F32),
        scratch_shapes=[pltpu.VMEM((D_MODEL, tn), BF16)],
        compiler_params=_params(("arbitrary", "arbitrary")),
        name="inproj",
    )(h, w_in)


CONV_PAD = 16
CONV_CHUNK = 64
GLU_CHUNK = 128


def _conv_kernel(ca_ref, cg_ref, w_ref, b_ref, g_ref, beta_ref, o_ref, upad_ref, *, seq):
    zeros = jnp.zeros((CONV_PAD, CONV_WIDTH), F32)
    upad_ref[pl.ds(0, CONV_PAD), :] = zeros
    upad_ref[pl.ds(seq + CONV_PAD, CONV_PAD), :] = zeros

    def glu(c, carry):
        r = pl.multiple_of(c * GLU_CHUNK, GLU_CHUNK)
        u = ca_ref[pl.ds(r, GLU_CHUNK), :] * jax.nn.sigmoid(cg_ref[pl.ds(r, GLU_CHUNK), :])
        upad_ref[pl.ds(r + CONV_PAD, GLU_CHUNK), :] = u
        return carry

    lax.fori_loop(0, seq // GLU_CHUNK, glu, 0)

    half = CONV_K // 2

    def conv(c, carry):
        r = pl.multiple_of(c * CONV_CHUNK, CONV_CHUNK)
        parts = []
        for lt in range(CONV_WIDTH // LANE):
            lanes = slice(lt * LANE, (lt + 1) * LANE)
            win = upad_ref[pl.ds(r, CONV_CHUNK + 2 * CONV_PAD), lanes]
            acc = jnp.zeros((CONV_CHUNK, LANE), F32) + b_ref[:, lanes]
            rows = CONV_CHUNK + 2 * CONV_PAD
            for s in range(SUBLANE):
                shifted = win if s == 0 else pltpu.roll(win, rows - s, axis=0)
                for a in range(2 * CONV_PAD // SUBLANE):
                    j = a * SUBLANE + s - (CONV_PAD - half)
                    if 0 <= j < CONV_K:
                        acc = acc + w_ref[j:j + 1, lanes] * shifted[a * SUBLANE:a * SUBLANE + CONV_CHUNK, :]
            parts.append(acc)
        acc = jnp.concatenate(parts, axis=1)
        mu = jnp.mean(acc, axis=-1, keepdims=True)
        d = acc - mu
        var = jnp.mean(d * d, axis=-1, keepdims=True)
        un = d * lax.rsqrt(var + EPS) * g_ref[...] + beta_ref[...]
        o_ref[pl.ds(r, CONV_CHUNK), :] = (un * jax.nn.sigmoid(un)).astype(BF16)
        return carry

    lax.fori_loop(0, seq // CONV_CHUNK, conv, 0)


def _conformer_conv(tok, p, dw_w, dw_b, ln_g, ln_b, layer, latent):
    seq = tok.s_lat if latent else tok.s_ctx
    nseq = tok.n_lat_seq if latent else tok.n_ctx_seq
    row0 = tok.t_ctx // seq if latent else 0
    vec = pl.BlockSpec((None, 1, CONV_WIDTH), lambda i: (layer, 0, 0))
    return pl.pallas_call(
        functools.partial(_conv_kernel, seq=seq),
        grid=(nseq,),
        in_specs=[
            pl.BlockSpec((seq, CONV_WIDTH), lambda i: (row0 + i, 0)),
            pl.BlockSpec((seq, CONV_WIDTH), lambda i: (row0 + i, 1)),
            pl.BlockSpec((None, CONV_K, CONV_WIDTH), lambda i: (layer, 0, 0)),
            vec, vec, vec,
        ],
        out_specs=pl.BlockSpec((seq, CONV_WIDTH), lambda i: (i, 0)),
        out_shape=jax.ShapeDtypeStruct((nseq * seq, CONV_WIDTH), BF16),
        scratch_shapes=[pltpu.VMEM((seq + 2 * CONV_PAD, CONV_WIDTH), F32)],
        compiler_params=_params(("parallel",)),
        name="conformer_conv_lat" if latent else "conformer_conv_ctx",
    )(p, p, dw_w, dw_b, ln_g, ln_b)


ATTN_SCALE = DA_QK_DIM ** -0.5
LOG2_E = math.log2(math.e)
ROPE_CHUNK = 256


def _rope(x, cos, sin_signed):
    lane = lax.broadcasted_iota(jnp.int32, x.shape, 1)
    first = (lane % DA_QK_DIM) < (DA_QK_DIM // 2)
    partner = jnp.where(first, pltpu.roll(x, LANE - DA_QK_DIM // 2, axis=1), pltpu.roll(x, DA_QK_DIM // 2, axis=1))
    return x * cos + partner * sin_signed


def _lambda(lamv_ref, lam_init):
    lv = lamv_ref[...]
    a = jnp.sum(lv[0:1, :] * lv[1:2, :], axis=-1, keepdims=True)
    b = jnp.sum(lv[2:3, :] * lv[3:4, :], axis=-1, keepdims=True)
    return jnp.exp(a) - jnp.exp(b) + lam_init


ATTN_ROWS = 128


def _diff_attn(q, kb, vb, lam, g, lam_init, rows=ATTN_ROWS):
    n = q.shape[0] // rows
    outs = [_diff_attn_rows(q[c * rows:(c + 1) * rows], kb, vb, lam, g, lam_init) for c in range(n)]
    return outs[0] if n == 1 else jnp.concatenate(outs, axis=0)


def _with_ones(vb):
    return jnp.concatenate([vb, jnp.ones_like(vb)], axis=1)


def _diff_attn_rows(q, kb, vb1, lam, g, lam_init):
    tq = q.shape[0]
    lane = lax.broadcasted_iota(jnp.int32, q.shape, 1)
    qs = q * (ATTN_SCALE * LOG2_E)
    q1 = jnp.where(lane < DA_QK_DIM, qs, 0.0).astype(BF16)
    q2 = jnp.where(lane >= DA_QK_DIM, qs, 0.0).astype(BF16)
    qq = jnp.concatenate([q1, q2], axis=0)
    s = lax.dot_general(qq, kb, (((1,), (1,)), ((), ())), preferred_element_type=F32)
    m = jnp.max(s, axis=-1, keepdims=True)
    e = jnp.exp2(s - m)
    ov = jnp.dot(e.astype(BF16), vb1, preferred_element_type=F32)
    ov = ov[:, :DA_V_DIM] * (1.0 / ov[:, DA_V_DIM:])
    o = ov[:tq] - lam * ov[tq:]
    y = o * lax.rsqrt(jnp.mean(o * o, axis=-1, keepdims=True) + EPS) * g
    return (y * (1.0 - lam_init)).astype(BF16)


def _attn_ctx_kernel(q_ref, k_ref, v_ref, lamv_ref, g_ref, o_ref, ko_ref, vo_ref, *, lam_init):
    lam = _lambda(lamv_ref, lam_init)
    for h in range(DA_HEADS):
        cols = slice(h * LANE, (h + 1) * LANE)
        k = k_ref[:, cols]
        v = v_ref[:, cols]
        ko_ref[:, h, :] = k
        vo_ref[:, h, :] = v
        o_ref[:, cols] = _diff_attn(q_ref[:, cols], k.astype(BF16), _with_ones(v.astype(BF16)), lam, g_ref[...],
                                    lam_init, rows=q_ref.shape[0])


def _attn_ctx(tok, p, da_lambda, da_norm, layer, lam_init):
    s = tok.s_ctx
    cache_shape = jax.ShapeDtypeStruct((tok.n_ctx_seq, s, DA_HEADS, DA_V_DIM), F32)
    cache_spec = pl.BlockSpec((None, s, DA_HEADS, DA_V_DIM), lambda b: (b, 0, 0, 0))
    return pl.pallas_call(
        functools.partial(_attn_ctx_kernel, lam_init=lam_init),
        grid=(tok.n_ctx_seq,),
        in_specs=[
            pl.BlockSpec((s, DA_WIDTH), lambda b: (b, COL_Q * LANE // DA_WIDTH)),
            pl.BlockSpec((s, DA_WIDTH), lambda b: (b, COL_K * LANE // DA_WIDTH)),
            pl.BlockSpec((s, DA_WIDTH), lambda b: (b, COL_V * LANE // DA_WIDTH)),
            pl.BlockSpec((None, 4, DA_QK_DIM), lambda b: (layer, 0, 0)),
            pl.BlockSpec((None, 1, DA_V_DIM), lambda b: (layer, 0, 0)),
        ],
        out_specs=[pl.BlockSpec((s, DA_WIDTH), lambda b: (b, 0)), cache_spec, cache_spec],
        out_shape=[jax.ShapeDtypeStruct((tok.t_ctx, DA_WIDTH), BF16), cache_shape, cache_shape],
        compiler_params=_params(("parallel",)),
        name="diff_attn_ctx",
    )(p, p, p, da_lambda, da_norm)


def _attn_lat_kernel(q_ref, k_ref, v_ref, ck_ref, cv_ref, cos_ref, sin_ref, lamv_ref, g_ref, o_ref,
                     kall_ref, vall_ref, *, lam_init, tq, s_lat):
    qi = pl.program_id(2)

    @pl.when(qi == 0)
    def _():
        def fill(c, carry):
            r = pl.multiple_of(c * ROPE_CHUNK, ROPE_CHUNK)
            rows = pl.ds(r, ROPE_CHUNK)
            kall_ref[rows, :] = _rope(k_ref[rows, :], cos_ref[rows, :], sin_ref[rows, :]).astype(BF16)
            vall_ref[rows, :] = _with_ones(v_ref[rows, :].astype(BF16))
            return carry

        lax.fori_loop(0, s_lat // ROPE_CHUNK, fill, 0)
        past = ck_ref.shape[0]
        kall_ref[pl.ds(s_lat, past), :] = ck_ref[...].astype(BF16)
        vall_ref[pl.ds(s_lat, past), :] = _with_ones(cv_ref[...].astype(BF16))

    rows = pl.ds(pl.multiple_of(qi * tq, tq), tq)
    q = _rope(q_ref[...], cos_ref[rows, :], sin_ref[rows, :])
    lam = _lambda(lamv_ref, lam_init)
    o_ref[...] = _diff_attn(q, kall_ref[...], vall_ref[...], lam, g_ref[...], lam_init)


def _attn_lat(tok, p, cache_k, cache_v, cos, sin, da_lambda, da_norm, layer, lam_init):
    s = tok.s_lat
    tq = min(1024, s)
    past = cache_k.shape[2]
    q0 = tok.t_ctx // tq
    kv0 = tok.t_ctx // s
    nq = s // tq
    return pl.pallas_call(
        functools.partial(_attn_lat_kernel, lam_init=lam_init, tq=tq, s_lat=s),
        grid=(tok.n_lat_seq, DA_HEADS, nq),
        in_specs=[
            pl.BlockSpec((tq, LANE), lambda b, h, i: (q0 + b * nq + i, COL_Q + h)),
            pl.BlockSpec((s, LANE), lambda b, h, i: (kv0 + b, COL_K + h)),
            pl.BlockSpec((s, LANE), lambda b, h, i: (kv0 + b, COL_V + h)),
            pl.BlockSpec((None, None, past, LANE), lambda b, h, i: (b, layer, 0, h)),
            pl.BlockSpec((None, None, past, LANE), lambda b, h, i: (b, layer, 0, h)),
            pl.BlockSpec((s, LANE), lambda b, h, i: (0, 0)),
            pl.BlockSpec((s, LANE), lambda b, h, i: (0, 0)),
            pl.BlockSpec((None, 4, DA_QK_DIM), lambda b, h, i: (layer, 0, 0)),
            pl.BlockSpec((None, 1, DA_V_DIM), lambda b, h, i: (layer, 0, 0)),
        ],
        out_specs=pl.BlockSpec((tq, LANE), lambda b, h, i: (b * nq + i, h)),
        out_shape=jax.ShapeDtypeStruct((tok.t_lat, DA_WIDTH), BF16),
        scratch_shapes=[pltpu.VMEM((s + past, LANE), BF16), pltpu.VMEM((s + past, 2 * DA_V_DIM), BF16)],
        compiler_params=_params(("parallel", "parallel", "arbitrary")),
        name="diff_attn_lat",
    )(p, p, p, cache_k, cache_v, cos, sin, da_lambda, da_norm)


RG_PAD = SUBLANE
RG_CHUNK = 128


def _scan_tile(a, u, hprev, reverse):
    row = lax.broadcasted_iota(jnp.int32, a.shape, 0)
    for k in (1, 2, 4):
        if reverse:
            keep = row < SUBLANE - k
            shift = SUBLANE - k
        else:
            keep = row >= k
            shift = k
        a_prev = jnp.where(keep, pltpu.roll(a, shift, axis=0), 1.0)
        u_prev = jnp.where(keep, pltpu.roll(u, shift, axis=0), 0.0)
        u = u + a * u_prev
        a = a * a_prev
    return a * hprev + u


def _rglru_kernel(rx_ref, rz_ref, cw_ref, cb_ref, wa_ref, ba_ref, wx_ref, bx_ref, lam_ref, h0_ref,
                  o_ref, last_ref, xpad_ref, a_ref, u_ref, hs_ref, *, seq):
    zeros = jnp.zeros((RG_PAD, RG_WIDTH), F32)
    xpad_ref[pl.ds(0, RG_PAD), :] = zeros
    xpad_ref[pl.ds(seq + RG_PAD, RG_PAD), :] = zeros

    def copy_in(c, carry):
        r = pl.multiple_of(c * RG_CHUNK, RG_CHUNK)
        xpad_ref[pl.ds(r + RG_PAD, RG_CHUNK), :] = rx_ref[pl.ds(r, RG_CHUNK), :]
        return carry

    lax.fori_loop(0, seq // RG_CHUNK, copy_in, 0)

    n_tiles = seq // SUBLANE
    for d in range(2):
        reverse = d == 1
        lam = lam_ref[d:d + 1, :]
        decay = -RG_C * (jnp.maximum(-lam, 0.0) + jnp.log(1.0 + jnp.exp(-jnp.abs(lam))))
        first_tap = RG_PAD if reverse else RG_PAD - (RG_CONV_K - 1)

        def gates(c, carry, d=d, decay=decay, first_tap=first_tap):
            r = pl.multiple_of(c * RG_CHUNK, RG_CHUNK)
            xc = jnp.zeros((RG_CHUNK, RG_WIDTH), F32) + cb_ref[d:d + 1, :]
            win = xpad_ref[pl.ds(r, RG_CHUNK + 2 * RG_PAD), :]
            for j in range(RG_CONV_K):
                xc = xc + cw_ref[d, j:j + 1, :] * win[first_tap + j:first_tap + j + RG_CHUNK, :]
            xb = xc.astype(BF16)
            rg = jax.nn.sigmoid(jnp.dot(xb, wa_ref[d], preferred_element_type=F32) + ba_ref[d:d + 1, :])
            ig = jax.nn.sigmoid(jnp.dot(xb, wx_ref[d], preferred_element_type=F32) + bx_ref[d:d + 1, :])
            log_a = rg * decay
            a = jnp.exp(log_a)
            a_ref[pl.ds(r, RG_CHUNK), :] = a
            u_ref[pl.ds(r, RG_CHUNK), :] = jnp.sqrt(1.0 - a * a) * ig * xc
            return carry

        lax.fori_loop(0, seq // RG_CHUNK, gates, 0)

        def scan(t, hprev, d=d, reverse=reverse):
            tile = (n_tiles - 1 - t) if reverse else t
            rows = pl.ds(pl.multiple_of(tile * SUBLANE, SUBLANE), SUBLANE)
            h = _scan_tile(a_ref[rows, :], u_ref[rows, :], hprev, reverse)
            if d == 0:
                hs_ref[rows, :] = h
            else:
                hs_ref[rows, :] = hs_ref[rows, :] + h
            return h[0:1, :] if reverse else h[SUBLANE - 1:SUBLANE, :]

        last = lax.fori_loop(0, n_tiles, scan, h0_ref[d:d + 1, :], unroll=4)
        last_ref[d:d + 1, :] = last

    def gate_out(c, carry):
        rows = pl.ds(pl.multiple_of(c * RG_CHUNK, RG_CHUNK), RG_CHUNK)
        o_ref[rows, :] = (hs_ref[rows, :] * jax.nn.gelu(rz_ref[rows, :])).astype(BF16)
        return carry

    lax.fori_loop(0, seq // RG_CHUNK, gate_out, 0)


def _rglru(tok, p, h0, h0_spec, rg, layer, latent):
    seq = tok.s_lat if latent else tok.s_ctx
    nseq = tok.n_lat_seq if latent else tok.n_ctx_seq
    row0 = tok.t_ctx // seq if latent else 0
    vec2 = pl.BlockSpec((None, 2, RG_WIDTH), lambda i: (layer, 0, 0))
    mat2 = pl.BlockSpec((None, 2, RG_WIDTH, RG_WIDTH), lambda i: (layer, 0, 0, 0))
    return pl.pallas_call(
        functools.partial(_rglru_kernel, seq=seq),
        grid=(nseq,),
        in_specs=[
            pl.BlockSpec((seq, RG_WIDTH), lambda i: (row0 + i, COL_RX)),
            pl.BlockSpec((seq, RG_WIDTH), lambda i: (row0 + i, COL_RZ)),
            pl.BlockSpec((None, 2, RG_CONV_K, RG_WIDTH), lambda i: (layer, 0, 0, 0)),
            vec2, mat2, vec2, mat2, vec2, vec2,
            h0_spec,
        ],
        out_specs=[
            pl.BlockSpec((seq, RG_WIDTH), lambda i: (i, 0)),
            pl.BlockSpec((None, 2, RG_WIDTH), lambda i: (i, 0, 0)),
        ],
        out_shape=[
            jax.ShapeDtypeStruct((nseq * seq, RG_WIDTH), BF16),
            jax.ShapeDtypeStruct((nseq, 2, RG_WIDTH), F32),
        ],
        scratch_shapes=[
            pltpu.VMEM((seq + 2 * RG_PAD, RG_WIDTH), F32),
            pltpu.VMEM((seq, RG_WIDTH), F32),
            pltpu.VMEM((seq, RG_WIDTH), F32),
            pltpu.VMEM((seq, RG_WIDTH), F32),
        ],
        compiler_params=_params(("parallel",)),
        name="rglru_lat" if latent else "rglru_ctx",
    )(p, p, rg["conv_w"], rg["conv_b"], rg["w_a"], rg["b_a"], rg["w_x"], rg["b_x"], rg["lam"], h0)


ROUTE_LANES = LANE


def _route(logits):
    lane = lax.broadcasted_iota(jnp.int32, logits.shape, 1).astype(F32)
    neg = -jnp.inf
    big = float(ROUTE_LANES)
    is_grp = lane < N_GROUPS
    gl = jnp.where(is_grp, logits, neg)
    gmax = jnp.max(gl, axis=-1, keepdims=True)
    grp = jnp.min(jnp.where(gl == gmax, lane, big), axis=-1, keepdims=True)
    gsum = jnp.sum(jnp.where(is_grp, jnp.exp(logits - gmax), 0.0), axis=-1, keepdims=True)
    p_grp = 1.0 / gsum
    lo = N_GROUPS + grp * EXPERTS_PER_GROUP
    in_grp = (lane >= lo) & (lane < lo + EXPERTS_PER_GROUP)
    el = jnp.where(in_grp, logits, neg)
    m1 = jnp.max(el, axis=-1, keepdims=True)
    i1 = jnp.min(jnp.where(el == m1, lane, big), axis=-1, keepdims=True)
    el2 = jnp.where(lane == i1, neg, el)
    m2 = jnp.max(el2, axis=-1, keepdims=True)
    i2 = jnp.min(jnp.where(el2 == m2, lane, big), axis=-1, keepdims=True)
    z = jnp.sum(jnp.where(in_grp, jnp.exp(logits - m1), 0.0), axis=-1, keepdims=True)
    w1 = 1.0 / z
    w2 = jnp.exp(m2 - m1) / z
    wsum = w1 + w2
    g1 = w1 / wsum * p_grp
    g2 = w2 / wsum * p_grp
    experts = jnp.where(lane == 0.0, i1 - N_GROUPS, jnp.where(lane == 1.0, i2 - N_GROUPS, 0.0))
    gates = jnp.where(lane == 0.0, g1, jnp.where(lane == 1.0, g2, 0.0))
    return experts.astype(jnp.int32), gates


def _outproj_kernel(conv_c, da_c, rg_c, conv_l, da_l, rg_l, x_ref, w_ref, gate1_ref, g2_ref, scale2_ref,
                    shift2_ref, wr_ref, br_ref, xo_ref, h2_ref, ei_ref, eg_ref, *, n_ctx_tiles):
    is_ctx = pl.program_id(0) < n_ctx_tiles
    conv = jnp.where(is_ctx, conv_c[...], conv_l[...])
    da = jnp.where(is_ctx, da_c[...], da_l[...])
    rg = jnp.where(is_ctx, rg_c[...], rg_l[...])
    mixed = jnp.dot(conv, w_ref[0:CONV_WIDTH, :], preferred_element_type=F32)
    mixed += jnp.dot(da, w_ref[CONV_WIDTH:CONV_WIDTH + DA_WIDTH, :], preferred_element_type=F32)
    mixed += jnp.dot(rg, w_ref[CONV_WIDTH + DA_WIDTH:MIX_WIDTH, :], preferred_element_type=F32)
    x = x_ref[...] + gate1_ref[...] * mixed
    xo_ref[...] = x
    h2 = _norm_mod(x, g2_ref[...], scale2_ref[...], shift2_ref[...])
    h2_ref[...] = _pack_rows(h2)
    logits = jnp.dot(h2.astype(BF16), wr_ref[...], preferred_element_type=F32) + br_ref[...]
    experts, gates = _route(logits)
    ei_ref[...] = experts
    eg_ref[...] = gates


def _outproj(tok, mix_ctx, mix_lat, x, w_out_b, norm2, mod5, w_route_b, b_route, layer):
    tm = 512
    n_ctx_tiles = tok.t_ctx // tm
    row = lambda width: pl.BlockSpec((tm, width), lambda i: (i, 0))
    ctx_row = lambda width: pl.BlockSpec((tm, width), lambda i: (jnp.minimum(i, n_ctx_tiles - 1), 0))
    lat_row = lambda width: pl.BlockSpec((tm, width), lambda i: (jnp.maximum(i - n_ctx_tiles, 0), 0))
    widths = (CONV_WIDTH, DA_WIDTH, RG_WIDTH)
    return pl.pallas_call(
        functools.partial(_outproj_kernel, n_ctx_tiles=n_ctx_tiles),
        grid=(tok.t // tm,),
        in_specs=[ctx_row(w) for w in widths] + [lat_row(w) for w in widths] + [
            row(D_MODEL),
            pl.BlockSpec((None, MIX_WIDTH, D_MODEL), lambda i: (layer, 0, 0), pipeline_mode=pl.Buffered(1)),
            tok.mod_spec(layer, 2, tm),
            pl.BlockSpec((None, 1, D_MODEL), lambda i: (layer, 0, 0)),
            tok.mod_spec(layer, 4, tm),
            tok.mod_spec(layer, 3, tm),
            pl.BlockSpec((None, D_MODEL, ROUTE_LANES), lambda i: (layer, 0, 0)),
            pl.BlockSpec((None, 1, ROUTE_LANES), lambda i: (layer, 0, 0)),
        ],
        out_specs=[row(D_MODEL), row(PACKED), row(ROUTE_LANES), row(ROUTE_LANES)],
        out_shape=[
            jax.ShapeDtypeStruct((tok.t, D_MODEL), F32),
            jax.ShapeDtypeStruct((tok.t, PACKED), U32),
            jax.ShapeDtypeStruct((tok.t, ROUTE_LANES), jnp.int32),
            jax.ShapeDtypeStruct((tok.t, ROUTE_LANES), F32),
        ],
        compiler_params=_params(("parallel",)),
        name="outproj_route",
    )(*mix_ctx, *mix_lat, x, w_out_b, mod5, norm2, mod5, mod5, w_route_b, b_route)


def _dispatch_plan(experts):
    t = experts.shape[0]
    a = t * TOP_K
    n_blocks = a // EXPERT_ROWS + N_EXPERTS
    ids = jnp.arange(N_EXPERTS, dtype=jnp.int32)
    e_flat = experts.reshape(a)
    order = jnp.argsort(e_flat, stable=True).astype(jnp.int32)
    pos = jnp.argsort(order).astype(jnp.int32)
    onehot = e_flat[:, None] == ids[None, :]
    counts = jnp.sum(onehot.astype(jnp.int32), axis=0)
    padded = (counts + EXPERT_ROWS - 1) // EXPERT_ROWS * EXPERT_ROWS
    start = jnp.cumsum(counts) - counts
    ends_p = jnp.cumsum(padded)
    shift = ends_p - padded - start
    dest = pos + jnp.sum(jnp.where(onehot, shift[None, :], 0), axis=1)
    blk_start = jnp.arange(n_blocks, dtype=jnp.int32) * EXPERT_ROWS
    blk_expert = jnp.minimum(jnp.sum((blk_start[:, None] >= ends_p[None, :]).astype(jnp.int32), axis=1),
                             N_EXPERTS - 1)
    src_off = blk_start - jnp.sum(jnp.where(blk_expert[:, None] == ids[None, :], shift[None, :], 0), axis=1)
    tok_sorted = jnp.concatenate([order // TOP_K, jnp.zeros((EXPERT_ROWS,), jnp.int32)])
    n_used = (ends_p[-1] // EXPERT_ROWS).reshape(1)
    return (blk_expert.astype(jnp.int32), n_used.astype(jnp.int32), src_off.astype(jnp.int32),
            tok_sorted.astype(jnp.int32), dest.astype(jnp.int32))


def _expert_kernel(bexp_ref, nused_ref, srcoff_ref, toks_ref, h2_hbm, wg_ref, wu_ref, wd_ref, y_ref,
                   xbuf, wgb_ref, wub_ref, wdb_ref, sem):
    b = pl.program_id(0)
    n_used = nused_ref[0]
    last_blk = n_used - 1

    def row_copy(base, r, slot):
        t = toks_ref[base + r]
        return pltpu.make_async_copy(h2_hbm.at[pl.ds(t, 1), :], xbuf.at[slot, pl.ds(r, 1), :], sem.at[slot])

    def issue(blk, slot, part=None):
        base = srcoff_ref[blk]
        group = EXPERT_ROWS // ISSUE_PARTS
        rows = range(EXPERT_ROWS) if part is None else range(part * group, (part + 1) * group)
        for r in rows:
            row_copy(base, r, slot).start(priority=r % 2)

    def drain(blk, slot):
        base = srcoff_ref[blk]

        def body(r, carry):
            row_copy(base, r, slot).wait()
            return carry
        lax.fori_loop(0, EXPERT_ROWS, body, 0, unroll=8)

    @pl.when(b == 0)
    def _():
        issue(0, 0)
        issue(jnp.minimum(1, last_blk), 1)

    new_expert = (b == 0) | (bexp_ref[b] != bexp_ref[jnp.maximum(b - 1, 0)])

    @pl.when((b < n_used) & new_expert)
    def _():
        wgb_ref[...] = wg_ref[...].astype(BF16)
        wub_ref[...] = wu_ref[...].astype(BF16)
        wdb_ref[...] = wd_ref[...].astype(BF16)

    for slot in range(2):
        @pl.when((b < n_used) & (b % 2 == slot))
        def _(slot=slot):
            drain(b, slot)
            nxt = jnp.minimum(b + 2, last_blk)
            x = _unpack_rows(xbuf[slot]).astype(BF16)
            issue(nxt, slot, 0)
            gate = jnp.dot(x, wgb_ref[...], preferred_element_type=F32)
            issue(nxt, slot, 1)
            up = jnp.dot(x, wub_ref[...], preferred_element_type=F32)
            issue(nxt, slot, 2)
            hid = (gate * jax.nn.sigmoid(gate)) * up
            y_ref[...] = _pack_rows(jnp.dot(hid.astype(BF16), wdb_ref[...], preferred_element_type=F32))
            issue(nxt, slot, 3)

    @pl.when(b == last_blk)
    def _():
        drain(last_blk, 0)
        drain(last_blk, 1)

    @pl.when(b >= n_used)
    def _():
        y_ref[...] = jnp.zeros_like(y_ref)


def _expert_ffn(h2, blk_expert, n_used, src_off, tok_sorted, w_gate, w_up, w_down, layer):
    n_blocks = blk_expert.shape[0]
    wspec_in = pl.BlockSpec((None, None, D_MODEL, D_EXPERT), lambda b, bexp, *_: (layer, bexp[b], 0, 0))
    wspec_out = pl.BlockSpec((None, None, D_EXPERT, D_MODEL), lambda b, bexp, *_: (layer, bexp[b], 0, 0))
    return pl.pallas_call(
        _expert_kernel,
        grid_spec=pltpu.PrefetchScalarGridSpec(
            num_scalar_prefetch=4,
            grid=(n_blocks,),
            in_specs=[pl.BlockSpec(memory_space=pl.ANY), wspec_in, wspec_in, wspec_out],
            out_specs=pl.BlockSpec((EXPERT_ROWS, PACKED), lambda b, *_: (b, 0)),
            scratch_shapes=[
                pltpu.VMEM((2, EXPERT_ROWS, PACKED), U32),
                pltpu.VMEM((D_MODEL, D_EXPERT), BF16),
                pltpu.VMEM((D_MODEL, D_EXPERT), BF16),
                pltpu.VMEM((D_EXPERT, D_MODEL), BF16),
                pltpu.SemaphoreType.DMA((2,)),
            ],
        ),
        out_shape=jax.ShapeDtypeStruct((n_blocks * EXPERT_ROWS, PACKED), U32),
        compiler_params=_params(("arbitrary",)),
        name="expert_ffn",
    )(blk_expert, n_used, src_off, tok_sorted, h2, w_gate, w_up, w_down)


def _combine_kernel(dest_ref, yb_hbm, x_ref, eg_ref, gate2_ref, g_ref, *rest, last, n_ctx_tiles):
    if last:
        yc_ref, yl_ref, ybuf, sem = rest
    else:
        scale_ref, shift_ref, xo_ref, h_ref, ybuf, sem = rest
    i = pl.program_id(0)
    last_tile = pl.num_programs(0) - 1
    rows = COMBINE_ROWS * TOP_K

    def row_copy(base, r, slot):
        d = dest_ref[base + r]
        return pltpu.make_async_copy(yb_hbm.at[pl.ds(d, 1), :], ybuf.at[slot, pl.ds(r, 1), :], sem.at[slot])

    def issue(tile, slot):
        base = tile * rows
        for r in range(rows):
            row_copy(base, r, slot).start(priority=r % 2)

    def drain(tile, slot):
        base = tile * rows

        def body(r, carry):
            row_copy(base, r, slot).wait()
            return carry
        lax.fori_loop(0, rows, body, 0, unroll=8)

    @pl.when(i == 0)
    def _():
        issue(0, 0)
        issue(jnp.minimum(1, last_tile), 1)

    for slot in range(2):
        @pl.when(i % 2 == slot)
        def _(slot=slot):
            drain(i, slot)
            y0 = _unpack_rows(ybuf[slot, 0:COMBINE_ROWS, :])
            y1 = _unpack_rows(ybuf[slot, COMBINE_ROWS:rows, :])
            eg = eg_ref[...]
            moe = y0 * eg[:, 0:1] + y1 * eg[:, 1:2]
            x = x_ref[...] + gate2_ref[...] * moe
            if last:
                y = x * lax.rsqrt(jnp.mean(x * x, axis=-1, keepdims=True) + EPS) * g_ref[...]

                @pl.when(i < n_ctx_tiles)
                def _():
                    yc_ref[...] = y

                @pl.when(i >= n_ctx_tiles)
                def _():
                    yl_ref[...] = y
            else:
                xo_ref[...] = x
                h_ref[...] = _norm_mod(x, g_ref[...], scale_ref[...], shift_ref[...]).astype(BF16)
            issue(jnp.minimum(i + 2, last_tile), slot)

    @pl.when(i == last_tile)
    def _():
        drain(last_tile, 0)
        drain(last_tile, 1)


def _combine(tok, yb, dest_tiles, x, eg, mod5, g, layer, last):
    tm = COMBINE_ROWS
    row = lambda width: pl.BlockSpec((tm, width), lambda i, d: (i, 0))
    in_specs = [pl.BlockSpec(memory_space=pl.ANY), row(D_MODEL), row(ROUTE_LANES), tok.mod_spec(layer, 5, tm)]
    args = [dest_tiles, yb, x, eg, mod5, g]
    n_ctx_tiles = tok.t_ctx // tm
    if last:
        in_specs += [pl.BlockSpec((1, D_MODEL), lambda i, d: (0, 0))]
        out_specs = [pl.BlockSpec((tm, D_MODEL), lambda i, d: (jnp.minimum(i, n_ctx_tiles - 1), 0)),
                     pl.BlockSpec((tm, D_MODEL), lambda i, d: (jnp.maximum(i - n_ctx_tiles, 0), 0))]
        out_shape = [jax.ShapeDtypeStruct((tok.t_ctx, D_MODEL), F32), jax.ShapeDtypeStruct((tok.t_lat, D_MODEL), F32)]
    else:
        in_specs += [pl.BlockSpec((None, 1, D_MODEL), lambda i, d: (layer + 1, 0, 0)),
                     tok.mod_spec(layer + 1, 1, tm), tok.mod_spec(layer + 1, 0, tm)]
        args += [mod5, mod5]
        out_specs = [row(D_MODEL), row(D_MODEL)]
        out_shape = [jax.ShapeDtypeStruct((tok.t, D_MODEL), F32), jax.ShapeDtypeStruct((tok.t, D_MODEL), BF16)]
    return pl.pallas_call(
        functools.partial(_combine_kernel, last=last, n_ctx_tiles=n_ctx_tiles),
        grid_spec=pltpu.PrefetchScalarGridSpec(
            num_scalar_prefetch=1,
            grid=(tok.t // tm,),
            in_specs=in_specs,
            out_specs=out_specs,
            scratch_shapes=[pltpu.VMEM((2, tm * TOP_K, PACKED), U32), pltpu.SemaphoreType.DMA((2,))],
        ),
        out_shape=out_shape,
        compiler_params=_params(("arbitrary",)),
        name="combine_final" if last else "combine_norm",
    )(*args)


def _rope_tables(n_tokens):
    pairs = DA_QK_DIM // 4
    rows = n_tokens // GRID_W
    row_ids = jnp.repeat(jnp.arange(rows, dtype=F32), GRID_W)
    col_ids = jnp.tile(jnp.arange(GRID_W, dtype=F32), rows)
    inv = 1.0 / (ROPE_BASE ** (jnp.arange(pairs, dtype=F32) / pairs))
    ang = jnp.concatenate([row_ids[:, None] * inv, col_ids[:, None] * inv], axis=-1)
    cos, sin = jnp.cos(ang), jnp.sin(ang)
    reps = LANE // DA_QK_DIM
    cos_full = jnp.tile(jnp.concatenate([cos, cos], axis=-1), (1, reps))
    sin_signed = jnp.tile(jnp.concatenate([-sin, sin], axis=-1), (1, reps))
    return cos_full, sin_signed


def _block_diag(w):
    eye = jnp.eye(RG_HEADS, dtype=w.dtype)
    return jnp.einsum("ldhij,hg->ldhigj", w, eye).reshape(DEPTH, 2, RG_WIDTH, RG_WIDTH)


def kernel(x_prompt, x_sample, c, c_ctx, cache_k, cache_v, state_rglru, w_mod, b_mod, norm1, norm2, w_in, w_out,
           conv_dw_w, conv_dw_b, conv_ln_g, conv_ln_b, da_lambda, da_norm, rg_conv_w, rg_conv_b, rg_w_a, rg_b_a,
           rg_w_x, rg_b_x, rg_lambda, moe_w_grp, moe_b_grp, moe_w_exp, moe_b_exp, moe_w_gate, moe_w_up,
           moe_w_down, final_norm):
    n_ctx_seq, s_ctx, d = x_prompt.shape
    n_lat_seq, s_lat, _ = x_sample.shape
    past = cache_k.shape[2]
    assert d == D_MODEL and w_in.shape == (DEPTH, D_MODEL, IN_WIDTH) and c.shape[0] + 1 <= N_COND
    assert s_lat % s_ctx == 0 and (n_ctx_seq * s_ctx) % s_lat == 0 and s_ctx % 256 == 0 and s_lat % 512 == 0
    tok = _Tokens(n_ctx_seq, s_ctx, n_lat_seq, s_lat)

    cond = jnp.concatenate([c_ctx[None, :], c, jnp.zeros((N_COND - 1 - n_lat_seq, d), F32)], axis=0)
    mod = _modulation(cond, w_mod, b_mod)
    mod5 = mod.reshape(DEPTH, N_COND, N_MOD, 1, D_MODEL)

    w_out_b = w_out.astype(BF16)
    rg = {
        "conv_w": rg_conv_w, "conv_b": rg_conv_b,
        "w_a": _block_diag(rg_w_a).astype(BF16), "b_a": rg_b_a,
        "w_x": _block_diag(rg_w_x).astype(BF16), "b_x": rg_b_x,
        "lam": rg_lambda,
    }
    w_route = jnp.concatenate(
        [moe_w_grp, moe_w_exp, jnp.zeros((DEPTH, d, ROUTE_LANES - N_GROUPS - N_EXPERTS), F32)], axis=-1)
    b_route = jnp.concatenate(
        [moe_b_grp, moe_b_exp, jnp.zeros((DEPTH, ROUTE_LANES - N_GROUPS - N_EXPERTS), F32)], axis=-1)
    b_route = b_route.reshape(DEPTH, 1, ROUTE_LANES)
    w_route_b = w_route.astype(BF16)
    norm1_r = norm1.reshape(DEPTH, 1, d)
    norm2_r = norm2.reshape(DEPTH, 1, d)
    dw_b = conv_dw_b.reshape(DEPTH, 1, CONV_WIDTH)
    ln_g = conv_ln_g.reshape(DEPTH, 1, CONV_WIDTH)
    ln_b = conv_ln_b.reshape(DEPTH, 1, CONV_WIDTH)
    da_norm_r = da_norm.reshape(DEPTH, 1, DA_V_DIM)
    cache_k_r = cache_k.reshape(n_lat_seq, DEPTH, past, DA_WIDTH)
    cache_v_r = cache_v.reshape(n_lat_seq, DEPTH, past, DA_WIDTH)
    cos, sin = _rope_tables(s_lat)
    h0_ctx = jnp.zeros((n_ctx_seq, 2, RG_WIDTH), F32)
    h0_ctx_spec = pl.BlockSpec((None, 2, RG_WIDTH), lambda i: (i, 0, 0))

    x = jnp.concatenate([x_prompt.reshape(tok.t_ctx, d), x_sample.reshape(tok.t_lat, d)], axis=0)
    h = _prenorm(tok, x, norm1_r, mod5, 0)
    new_k, new_v, new_h = [], [], []
    for layer in range(DEPTH):
        lam_init = 0.8 - 0.6 * math.exp(-0.3 * layer)
        p = _inproj(tok, h, w_in, layer)
        conv_ctx = _conformer_conv(tok, p, conv_dw_w, dw_b, ln_g, ln_b, layer, False)
        conv_lat = _conformer_conv(tok, p, conv_dw_w, dw_b, ln_g, ln_b, layer, True)
        da_ctx, k_l, v_l = _attn_ctx(tok, p, da_lambda, da_norm_r, layer, lam_init)
        new_k.append(k_l)
        new_v.append(v_l)
        da_lat = _attn_lat(tok, p, cache_k_r, cache_v_r, cos, sin, da_lambda, da_norm_r, layer, lam_init)
        rg_ctx, last_ctx = _rglru(tok, p, h0_ctx, h0_ctx_spec, rg, layer, False)
        h0_lat_spec = pl.BlockSpec((None, None, 2, RG_WIDTH), lambda i, layer=layer: (i, layer, 0, 0))
        rg_lat, _ = _rglru(tok, p, state_rglru, h0_lat_spec, rg, layer, True)
        new_h.append(last_ctx)

        x, h2, ei, eg = _outproj(tok, (conv_ctx, da_ctx, rg_ctx), (conv_lat, da_lat, rg_lat), x, w_out_b, norm2_r,
                                 mod5, w_route_b, b_route, layer)

        blk_expert, n_used, src_off, tok_sorted, dest = _dispatch_plan(ei[:, :TOP_K])
        yb = _expert_ffn(h2, blk_expert, n_used, src_off, tok_sorted, moe_w_gate, moe_w_up, moe_w_down, layer)
        dest_tiles = dest.reshape(tok.t // COMBINE_ROWS, COMBINE_ROWS, TOP_K).transpose(0, 2, 1).reshape(-1)
        last = layer == DEPTH - 1
        if last:
            y_ctx, y_lat = _combine(tok, yb, dest_tiles, x, eg, mod5, final_norm.reshape(1, d), layer, True)
        else:
            x, h = _combine(tok, yb, dest_tiles, x, eg, mod5, norm1_r, layer, False)

    y_prompt = y_ctx.reshape(n_ctx_seq, s_ctx, d)
    y_sample = y_lat.reshape(n_lat_seq, s_lat, d)
    return (y_prompt, y_sample, jnp.stack(new_k, axis=1), jnp.stack(new_v, axis=1), jnp.stack(new_h, axis=1))
```
